```python
import math
import jax, jax.numpy as jnp
from jax import lax
import numpy as np

D_MODEL = 1024
BATCH = 8
SEQ = 2048
DEPTH = 2

N_MIXERS = 2
N_RET_LAYERS = (DEPTH + 1) // 2
N_MLA_LAYERS = DEPTH // 2
DN_ALPHA = (2.0 * DEPTH) ** 0.25
DN_BETA = (8.0 * DEPTH) ** -0.25
LN_EPS = 1e-5
ROPE_BASE = 10000.0

RET_HEADS = 4
RET_DK = D_MODEL // RET_HEADS
RET_DV = 2 * RET_DK
RET_HK = RET_HEADS * RET_DK
RET_HV = RET_HEADS * RET_DV
RET_IN = 2 * RET_HK + 2 * RET_HV
RET_CHUNK = 128

MLA_HEADS = 8
MLA_NOPE = 128
MLA_ROPE = 64
MLA_VDIM = 128
MLA_Q_RANK = 384
MLA_KV_RANK = 256
MLA_IN = MLA_Q_RANK + MLA_KV_RANK + MLA_ROPE
MLA_QBLOCK = 128

PEER_HEADS = 8
PEER_NKEYS = 128
PEER_EXPERTS = PEER_NKEYS * PEER_NKEYS
PEER_DKEY = 256
PEER_HALF = PEER_DKEY // 2
PEER_TOPK = 16
PEER_TOKEN_BLOCK = 128

kernel_name = 'hybrid_retention_mla_peer_deepnorm'


def layer_norm(x, g, b):
    xf = x.astype(jnp.float32)
    mu = jnp.mean(xf, axis=-1, keepdims=True)
    var = jnp.mean(jnp.square(xf - mu), axis=-1, keepdims=True)
    return ((xf - mu) * lax.rsqrt(var + LN_EPS) * g.astype(jnp.float32) + b.astype(jnp.float32)).astype(x.dtype)


def rms_norm(x, g):
    xf = x.astype(jnp.float32)
    ms = jnp.mean(jnp.square(xf), axis=-1, keepdims=True)
    return (xf * lax.rsqrt(ms + LN_EPS) * g.astype(jnp.float32)).astype(x.dtype)


def rotary(x, positions):
    d = x.shape[-1]
    inv_freq = ROPE_BASE ** (-jnp.arange(0, d, 2, dtype=jnp.float32) / d)
    ang = positions.astype(jnp.float32)[..., None] * inv_freq
    cos = jnp.cos(ang)[:, :, None, :]
    sin = jnp.sin(ang)[:, :, None, :]
    x1, x2 = jnp.split(x.astype(jnp.float32), 2, axis=-1)
    return jnp.concatenate([x1 * cos - x2 * sin, x1 * sin + x2 * cos], axis=-1).astype(x.dtype)


def retention_scan(q, k, v, log_gamma, strict):
    B, S, H, dk = q.shape
    dv = v.shape[-1]
    C = RET_CHUNK
    n_chunks = S // C
    dt = q.dtype
    idx = jnp.arange(C, dtype=jnp.float32)
    diff = idx[:, None] - idx[None, :]
    keep = (diff > 0) if strict else (diff >= 0)
    lg = log_gamma[:, None, None]
    decay = jnp.where(keep[None], jnp.exp(jnp.maximum(diff, 0.0)[None] * lg), 0.0).astype(dt)
    xi = jnp.exp((idx + 1.0)[None, :] * log_gamma[:, None]).astype(dt)
    zeta = jnp.exp((C - 1.0 - idx)[None, :] * log_gamma[:, None]).astype(dt)
    gamma_c = jnp.exp(C * log_gamma).astype(dt)[None, :, None, None]

    def to_chunks(t):
        return t.reshape(B, n_chunks, C, H, t.shape[-1]).transpose(1, 0, 3, 2, 4)

    def step(state, inp):
        qi, ki, vi = inp
        inner = jnp.einsum('bhnm,bhmv->bhnv', jnp.einsum('bhnd,bhmd->bhnm', qi, ki) * decay[None], vi)
        cross = jnp.einsum('bhnd,bhdv->bhnv', qi, state) * xi[None, :, :, None]
        state = gamma_c * state + jnp.einsum('bhmd,bhmv->bhdv', ki * zeta[None, :, :, None], vi)
        return state, inner + cross

    state0 = jnp.zeros((B, H, dk, dv), dt)
    _, out = lax.scan(step, state0, (to_chunks(q), to_chunks(k), to_chunks(v)))
    return out.transpose(1, 0, 3, 2, 4).reshape(B, S, H, dv)


def retention_mixer(x, positions, w_in, log1m_decay, gn_g, gn_b, w_out):
    B, S, _ = x.shape
    proj = x @ w_in
    q, k, v, g = jnp.split(proj, [RET_HK, 2 * RET_HK, 2 * RET_HK + RET_HV], axis=-1)
    q = rotary(q.reshape(B, S, RET_HEADS, RET_DK), positions)
    k = rotary(k.reshape(B, S, RET_HEADS, RET_DK), positions) * (RET_DK ** -0.5)
    v = v.reshape(B, S, RET_HEADS, RET_DV)
    log_gamma = jnp.log1p(-jnp.exp(log1m_decay.astype(jnp.float32)))
    fwd = retention_scan(q, k, v, log_gamma[0], strict=False)
    bwd = jnp.flip(retention_scan(jnp.flip(q, 1), jnp.flip(k, 1), jnp.flip(v, 1), log_gamma[1], strict=True), 1)
    y = (fwd + bwd).astype(jnp.float32)
    mu = jnp.mean(y, axis=-1, keepdims=True)
    var = jnp.mean(jnp.square(y - mu), axis=-1, keepdims=True)
    yn = ((y - mu) * lax.rsqrt(var + LN_EPS)).reshape(B, S, RET_HV)
    yn = (yn * gn_g.astype(jnp.float32) + gn_b.astype(jnp.float32)).astype(x.dtype)
    return (jax.nn.silu(g) * yn) @ w_out


def mla_mixer(x, positions, w_in, q_norm_g, kv_norm_g, w_uq, w_ukv, w_out):
    B, S, _ = x.shape
    c_q, c_kv, k_rope = jnp.split(x @ w_in, [MLA_Q_RANK, MLA_Q_RANK + MLA_KV_RANK], axis=-1)
    q = (rms_norm(c_q, q_norm_g) @ w_uq).reshape(B, S, MLA_HEADS, MLA_NOPE + MLA_ROPE)
    q_nope, q_rope = jnp.split(q, [MLA_NOPE], axis=-1)
    q_rope = rotary(q_rope, positions)
    kv = (rms_norm(c_kv, kv_norm_g) @ w_ukv).reshape(B, S, MLA_HEADS, MLA_NOPE + MLA_VDIM)
    k_nope, v = jnp.split(kv, [MLA_NOPE], axis=-1)
    k_rope = rotary(k_rope[:, :, None, :], positions)[:, :, 0, :]
    scale = (MLA_NOPE + MLA_ROPE) ** -0.5
    nb = S // MLA_QBLOCK

    def blocks(t):
        return t.reshape(B, nb, MLA_QBLOCK, MLA_HEADS, t.shape[-1]).transpose(1, 0, 2, 3, 4)

    def attend(blk):
        qn, qr = blk
        s = jnp.einsum('bqhd,bkhd->bhqk', qn, k_nope) + jnp.einsum('bqhd,bkd->bhqk', qr, k_rope)
        p = jax.nn.softmax(s.astype(jnp.float32) * scale, axis=-1).astype(v.dtype)
        return jnp.einsum('bhqk,bkhd->bqhd', p, v)

    o = lax.map(attend, (blocks(q_nope), blocks(q_rope)))
    o = o.transpose(1, 0, 2, 3, 4).reshape(B, S, MLA_HEADS * MLA_VDIM)
    return o @ w_out


def peer_ffn(x, w_q, sub_keys, u, v):
    B, S, D = x.shape
    T = B * S
    xt = x.reshape(T // PEER_TOKEN_BLOCK, PEER_TOKEN_BLOCK, D)

    def block(xb):
        q = (xb @ w_q).reshape(PEER_TOKEN_BLOCK, PEER_HEADS, 2, PEER_HALF)
        s = jnp.einsum('thcd,hckd->thck', q, sub_keys).astype(jnp.float32)
        s1, i1 = lax.top_k(s[:, :, 0], PEER_TOPK)
        s2, i2 = lax.top_k(s[:, :, 1], PEER_TOPK)
        cand_s = (s1[..., :, None] + s2[..., None, :]).reshape(PEER_TOKEN_BLOCK, PEER_HEADS, PEER_TOPK * PEER_TOPK)
        cand_i = (i1[..., :, None] * PEER_NKEYS + i2[..., None, :]).reshape(PEER_TOKEN_BLOCK, PEER_HEADS, PEER_TOPK * PEER_TOPK)
        top_s, pos = lax.top_k(cand_s, PEER_TOPK)
        eidx = jnp.take_along_axis(cand_i, pos, axis=-1)
        gate = jax.nn.softmax(top_s, axis=-1).astype(xb.dtype)
        h = jax.nn.gelu(jnp.einsum('thkd,td->thk', u[eidx], xb), approximate=False)
        return jnp.einsum('thk,thkd->td', gate * h, v[eidx])

    return lax.map(block, xt).reshape(B, S, D)


def setup_inputs(seed: int = 0) -> dict:
    key = jax.random.key(seed)
    ks = jax.random.split(key, 24)
    f32 = jnp.float32
    nrm = lambda k, shape, s: jax.random.normal(k, shape, f32) * s
    x = jax.random.normal(ks[0], (BATCH, SEQ, D_MODEL), f32)
    offsets = jax.random.randint(ks[1], (BATCH, 1), 0, 512, dtype=jnp.int32)
    positions = (offsets + jnp.arange(SEQ, dtype=jnp.int32)[None, :]).astype(jnp.int32)
    ret_w_in = nrm(ks[2], (N_RET_LAYERS, D_MODEL, RET_IN), D_MODEL ** -0.5)
    base = (-5.0 - jnp.arange(RET_HEADS, dtype=f32)) * math.log(2.0)
    ret_log1m_decay = base[None, None, :] + nrm(ks[3], (N_RET_LAYERS, 2, RET_HEADS), 0.05)
    ret_gn_g = 1.0 + nrm(ks[4], (N_RET_LAYERS, RET_HV), 0.02)
    ret_gn_b = nrm(ks[5], (N_RET_LAYERS, RET_HV), 0.02)
    ret_w_out = nrm(ks[6], (N_RET_LAYERS, RET_HV, D_MODEL), DN_BETA * RET_HV ** -0.5)
    mla_w_in = nrm(ks[7], (N_MLA_LAYERS, D_MODEL, MLA_IN), D_MODEL ** -0.5)
    mla_q_norm = 1.0 + nrm(ks[8], (N_MLA_LAYERS, MLA_Q_RANK), 0.02)
    mla_kv_norm = 1.0 + nrm(ks[9], (N_MLA_LAYERS, MLA_KV_RANK), 0.02)
    mla_w_uq = nrm(ks[10], (N_MLA_LAYERS, MLA_Q_RANK, MLA_HEADS * (MLA_NOPE + MLA_ROPE)), MLA_Q_RANK ** -0.5)
    mla_w_ukv = nrm(ks[11], (N_MLA_LAYERS, MLA_KV_RANK, MLA_HEADS * (MLA_NOPE + MLA_VDIM)), MLA_KV_RANK ** -0.5)
    mla_w_out = nrm(ks[12], (N_MLA_LAYERS, MLA_HEADS * MLA_VDIM, D_MODEL), DN_BETA * (MLA_HEADS * MLA_VDIM) ** -0.5)
    peer_w_q = nrm(ks[13], (DEPTH, D_MODEL, PEER_HEADS * PEER_DKEY), D_MODEL ** -0.5)
    peer_sub_keys = nrm(ks[14], (DEPTH, PEER_HEADS, 2, PEER_NKEYS, PEER_HALF), PEER_HALF ** -0.5)
    peer_u = nrm(ks[15], (DEPTH, PEER_EXPERTS, D_MODEL), D_MODEL ** -0.5)
    peer_v = nrm(ks[16], (DEPTH, PEER_EXPERTS, D_MODEL), DN_BETA * (PEER_HEADS * PEER_TOPK) ** -0.5)
    ln_mix_g = 1.0 + nrm(ks[17], (DEPTH, D_MODEL), 0.02)
    ln_mix_b = nrm(ks[18], (DEPTH, D_MODEL), 0.02)
    ln_ffn_g = 1.0 + nrm(ks[19], (DEPTH, D_MODEL), 0.02)
    ln_ffn_b = nrm(ks[20], (DEPTH, D_MODEL), 0.02)
    return {'x': x, 'positions': positions,
            'ret_w_in': ret_w_in, 'ret_log1m_decay': ret_log1m_decay, 'ret_gn_g': ret_gn_g, 'ret_gn_b': ret_gn_b, 'ret_w_out': ret_w_out,
            'mla_w_in': mla_w_in, 'mla_q_norm': mla_q_norm, 'mla_kv_norm': mla_kv_norm, 'mla_w_uq': mla_w_uq, 'mla_w_ukv': mla_w_ukv, 'mla_w_out': mla_w_out,
            'peer_w_q': peer_w_q, 'peer_sub_keys': peer_sub_keys, 'peer_u': peer_u, 'peer_v': peer_v,
            'ln_mix_g': ln_mix_g, 'ln_mix_b': ln_mix_b, 'ln_ffn_g': ln_ffn_g, 'ln_ffn_b': ln_ffn_b}


def reference(x, positions,
              ret_w_in, ret_log1m_decay, ret_gn_g, ret_gn_b, ret_w_out,
              mla_w_in, mla_q_norm, mla_kv_norm, mla_w_uq, mla_w_ukv, mla_w_out,
              peer_w_q, peer_sub_keys, peer_u, peer_v,
              ln_mix_g, ln_mix_b, ln_ffn_g, ln_ffn_b):
    h = x
    for i in range(DEPTH):
        j = i // N_MIXERS
        if i % N_MIXERS == 0:
            mix = retention_mixer(h, positions, ret_w_in[j], ret_log1m_decay[j], ret_gn_g[j], ret_gn_b[j], ret_w_out[j])
        else:
            mix = mla_mixer(h, positions, mla_w_in[j], mla_q_norm[j], mla_kv_norm[j], mla_w_uq[j], mla_w_ukv[j], mla_w_out[j])
        h = layer_norm(DN_ALPHA * h + mix, ln_mix_g[i], ln_mix_b[i])
        ffn = peer_ffn(h, peer_w_q[i], peer_sub_keys[i], peer_u[i], peer_v[i])
        h = layer_norm(DN_ALPHA * h + ffn, ln_ffn_g[i], ln_ffn_b[i])
    return h
```

```python
import functools
import math

import jax
import jax.numpy as jnp
from jax import lax
from jax.experimental import pallas as pl
from jax.experimental.pallas import tpu as pltpu

F32 = jnp.float32
BF16 = jnp.bfloat16
I32 = jnp.int32
U32 = jnp.uint32

D_MODEL = 1024
DEPTH = 2
DN_ALPHA = (2.0 * DEPTH) ** 0.25
LN_EPS = 1e-5

PEER_HEADS = 8
PEER_NKEYS = 128
PEER_HALF = 128
PEER_TOPK = 16
PEER_SLOTS = PEER_HEADS * PEER_TOPK
PEER_EXPERTS = PEER_NKEYS * PEER_NKEYS
PEER_HALF_EXPERTS = PEER_EXPERTS // 2

SUBLANES = 8
LANES = 128
VMEM_LIMIT_BYTES = 56 * 1024 * 1024

SEL_TOKENS = 256
GATHER_TOKENS = 64


def _top_rows(st, ids, k, payload=None):
    vals, sel_ids, sel_pay = [], [], []
    big = jnp.int32(2 ** 30)
    for _ in range(k):
        m = jnp.max(st, axis=0, keepdims=True)
        sel = jnp.min(jnp.where(st == m, ids, big), axis=0, keepdims=True)
        hit = ids == sel
        vals.append(m)
        sel_ids.append(sel)
        if payload is not None:
            sel_pay.append(jnp.max(jnp.where(hit, payload, -1), axis=0, keepdims=True))
        st = jnp.where(hit, -jnp.inf, st)
    return vals, sel_ids, sel_pay


def _peer_select_kernel(h_ref, wq_ref, keys_ref, eidx_ref, gate_ref, s_buf, i_buf):
    tb = h_ref.shape[0]
    q = jnp.dot(h_ref[...].astype(BF16), wq_ref[...], preferred_element_type=F32)
    key_ids = lax.broadcasted_iota(I32, (PEER_NKEYS, tb), 0)
    a_ids = lax.broadcasted_iota(I32, (PEER_TOPK, tb), 0)
    for head in range(PEER_HEADS):
        for c in range(2):
            col = (head * 2 + c) * PEER_HALF
            qhc = q[:, col:col + PEER_HALF].astype(BF16)
            st = lax.dot_general(keys_ref[head, c], qhc, (((1,), (1,)), ((), ())),
                                 preferred_element_type=F32)
            vals, ids, _ = _top_rows(st, key_ids, PEER_TOPK)
            for a in range(PEER_TOPK):
                s_buf[c, pl.ds(a, 1), :] = vals[a]
                i_buf[c, pl.ds(a, 1), :] = ids[a]
        s1 = s_buf[0]
        i1 = i_buf[0]
        cand_s, cand_pos, cand_e = [], [], []
        for b in range(PEER_TOPK):
            cand_s.append(s1 + s_buf[1, pl.ds(b, 1), :])
            cand_pos.append(a_ids * PEER_TOPK + b)
            cand_e.append(i1 * PEER_NKEYS + i_buf[1, pl.ds(b, 1), :])
        cand_s = jnp.concatenate(cand_s, axis=0)
        cand_pos = jnp.concatenate(cand_pos, axis=0)
        cand_e = jnp.concatenate(cand_e, axis=0)
        top_s, _, top_e = _top_rows(cand_s, cand_pos, PEER_TOPK, payload=cand_e)
        ex = [jnp.exp(t - top_s[0]) for t in top_s]
        denom = functools.reduce(lambda x, y: x + y, ex)
        inv = 1.0 / denom
        for k in range(PEER_TOPK):
            slot = head * PEER_TOPK + k
            eidx_ref[pl.ds(slot, 1), :] = top_e[k]
            gate_ref[pl.ds(slot, 1), :] = ex[k] * inv


def _peer_select(h, w_q, sub_keys):
    t = h.shape[0]
    tb = SEL_TOKENS
    wq = w_q.astype(BF16)
    keys = sub_keys.astype(BF16)
    eidx_t, gate_t = pl.pallas_call(
        _peer_select_kernel,
        grid=(t // tb,),
        in_specs=[pl.BlockSpec((tb, D_MODEL), lambda i: (i, 0)),
                  pl.BlockSpec(wq.shape, lambda i: (0, 0)),
                  pl.BlockSpec(keys.shape, lambda i: (0, 0, 0, 0))],
        out_specs=[pl.BlockSpec((PEER_SLOTS, tb), lambda i: (0, i)),
                   pl.BlockSpec((PEER_SLOTS, tb), lambda i: (0, i))],
        out_shape=[jax.ShapeDtypeStruct((PEER_SLOTS, t), I32),
                   jax.ShapeDtypeStruct((PEER_SLOTS, t), F32)],
        scratch_shapes=[pltpu.VMEM((2, PEER_TOPK, tb), F32), pltpu.VMEM((2, PEER_TOPK, tb), I32)],
        compiler_params=pltpu.CompilerParams(dimension_semantics=("arbitrary",),
                                             vmem_limit_bytes=VMEM_LIMIT_BYTES),
        name="peer_select",
    )(h, wq, keys)
    return eidx_t.T, gate_t.T


def _pack_table(w):
    bits = lax.bitcast_convert_type(w.astype(BF16), jnp.uint16).astype(U32)
    packed = bits[:PEER_HALF_EXPERTS] | (bits[PEER_HALF_EXPERTS:] << 16)
    return packed.reshape(PEER_HALF_EXPERTS * SUBLANES, LANES)


def _expert_row(tbl_ref, row8, shift):
    w = tbl_ref[pl.ds(pl.multiple_of(row8, SUBLANES), SUBLANES), :]
    return pltpu.bitcast((w << shift.astype(U32)) & jnp.uint32(0xFFFF0000), F32)


def _rowsum_transpose(ps, sub):
    m4 = sub < 4
    lvl1 = []
    for j in range(4):
        x, y = ps[j], ps[j + 4]
        lvl1.append(jnp.where(m4, x, y) + pltpu.roll(jnp.where(m4, y, x), 4, 0))
    m2 = (sub & 2) == 0
    lvl2 = []
    for j in range(2):
        x, y = lvl1[j], lvl1[j + 2]
        lvl2.append(jnp.where(m2, x, pltpu.roll(y, 2, 0)) + jnp.where(m2, pltpu.roll(x, 6, 0), y))
    m1 = (sub & 1) == 0
    x, y = lvl2[0], lvl2[1]
    r = jnp.where(m1, x, pltpu.roll(y, 1, 0)) + jnp.where(m1, pltpu.roll(x, 7, 0), y)
    return r


def _peer_u_kernel(row_ref, sh_ref, x_ref, gate_ref, tbl_ref, c_ref):
    tt = x_ref.shape[0]
    n_chain = 4
    sub = lax.broadcasted_iota(I32, (SUBLANES, LANES), 0)
    lane = lax.broadcasted_iota(I32, (SUBLANES, LANES), 1)

    def group(g, carry):
        t0 = pl.multiple_of(g * SUBLANES, SUBLANES)
        xs = [x_ref[t0 + j] for j in range(SUBLANES)]
        base = g * (PEER_SLOTS * SUBLANES)
        hs = [jnp.zeros((SUBLANES, LANES), F32) for _ in range(n_chain)]
        for p in range(PEER_SLOTS):
            ps = []
            for j in range(SUBLANES):
                i = base + (p * SUBLANES + j)
                ps.append(_expert_row(tbl_ref, row_ref[i], sh_ref[i]) * xs[j])
            dot = jnp.sum(_rowsum_transpose(ps, sub), axis=1, keepdims=True)
            hs[p % n_chain] = jnp.where(lane == p, dot, hs[p % n_chain])
        hdot = (hs[0] + hs[1]) + (hs[2] + hs[3])
        act = 0.5 * hdot * (1.0 + lax.erf(hdot * (2.0 ** -0.5)))
        rows = pl.ds(t0, SUBLANES)
        c_ref[rows, :] = act * gate_ref[rows, :]
        return carry

    lax.fori_loop(0, tt // SUBLANES, group, 0)


def _peer_v_kernel(code_ref, tbl_ref, o_ref):
    tt = o_ref.shape[0]
    n_acc = 4

    def token(t, carry):
        base = t * PEER_SLOTS
        accs = [jnp.zeros((SUBLANES, LANES), F32) for _ in range(n_acc)]
        for p in range(PEER_SLOTS):
            code = code_ref[base + p]
            row8 = pl.multiple_of(code & 0xFFF8, SUBLANES)
            cv = jnp.full((SUBLANES, LANES), code, I32)
            shift = ((cv & 1) ^ 1) << 4
            coef = pltpu.bitcast(cv & jnp.int32(-65536), F32)
            w = pltpu.bitcast(tbl_ref[pl.ds(row8, SUBLANES), :], I32)
            f = pltpu.bitcast((w << shift) & jnp.int32(-65536), F32)
            accs[p % n_acc] = accs[p % n_acc] + coef * f
        o_ref[t] = (accs[0] + accs[1]) + (accs[2] + accs[3])
        return carry

    lax.fori_loop(0, tt, token, 0)


def _peer_retrieve(h, eidx, gate, u_tbl, v_tbl):
    t = h.shape[0]
    tt = GATHER_TOKENS
    row8 = (eidx & (PEER_HALF_EXPERTS - 1)) * SUBLANES
    upper = (eidx >= PEER_HALF_EXPERTS).astype(I32)

    def grouped(a):
        return a.reshape(t // SUBLANES, SUBLANES, PEER_SLOTS).transpose(0, 2, 1).reshape(-1)

    x3 = h.reshape(t, SUBLANES, LANES)
    smem_spec = pl.BlockSpec((tt * PEER_SLOTS,), lambda i: (i,), memory_space=pltpu.SMEM)
    vmem_slots = pl.BlockSpec((tt, PEER_SLOTS), lambda i: (i, 0))
    tile3 = pl.BlockSpec((tt, SUBLANES, LANES), lambda i: (i, 0, 0))
    table_spec = pl.BlockSpec(memory_space=pltpu.VMEM)
    params = pltpu.CompilerParams(dimension_semantics=("arbitrary",), vmem_limit_bytes=VMEM_LIMIT_BYTES)
    coef = pl.pallas_call(
        _peer_u_kernel,
        grid=(t // tt,),
        in_specs=[smem_spec, smem_spec, tile3, vmem_slots, table_spec],
        out_specs=vmem_slots,
        out_shape=jax.ShapeDtypeStruct((t, PEER_SLOTS), F32),
        compiler_params=params,
        name="peer_u",
    )(grouped(row8), grouped(16 - 16 * upper), x3, gate, u_tbl)
    coef_bits = lax.bitcast_convert_type(coef.astype(BF16), jnp.uint16).astype(I32) << 16
    code = (coef_bits | row8 | upper).reshape(-1)
    out3 = pl.pallas_call(
        _peer_v_kernel,
        grid=(t // tt,),
        in_specs=[smem_spec, table_spec],
        out_specs=tile3,
        out_shape=jax.ShapeDtypeStruct((t, SUBLANES, LANES), F32),
        compiler_params=params,
        name="peer_v",
    )(code, v_tbl)
    return out3.reshape(t, D_MODEL)


def _add_ln_kernel(h_ref, f_ref, g_ref, b_ref, o_ref):
    y = DN_ALPHA * h_ref[...] + f_ref[...]
    mu = jnp.mean(y, axis=-1, keepdims=True)
    d = y - mu
    var = jnp.mean(d * d, axis=-1, keepdims=True)
    o_ref[...] = d * lax.rsqrt(var + LN_EPS) * g_ref[...] + b_ref[...]


def _add_ln(h, f, g, b):
    t, d = h.shape
    tb = 512
    row = pl.BlockSpec((tb, d), lambda i: (i, 0))
    vec = pl.BlockSpec((1, d), lambda i: (0, 0))
    return pl.pallas_call(
        _add_ln_kernel,
        grid=(t // tb,),
        in_specs=[row, row, vec, vec],
        out_specs=row,
        out_shape=jax.ShapeDtypeStruct((t, d), F32),
        compiler_params=pltpu.CompilerParams(dimension_semantics=("arbitrary",)),
        name="add_ln",
    )(h, f, g.reshape(1, d), b.reshape(1, d))


def _peer_ffn(h, w_q, sub_keys, u, v):
    eidx, gate = _peer_select(h, w_q, sub_keys)
    return _peer_retrieve(h, eidx, gate, _pack_table(u), _pack_table(v))


ROW_TILE = 512


def _proj_ln_kernel(a_ref, w_ref, h_ref, g_ref, b_ref, o_ref):
    y = DN_ALPHA * h_ref[...] + jnp.dot(a_ref[...], w_ref[...], preferred_element_type=F32)
    mu = jnp.mean(y, axis=-1, keepdims=True)
    d = y - mu
    var = jnp.mean(d * d, axis=-1, keepdims=True)
    o_ref[...] = d * lax.rsqrt(var + LN_EPS) * g_ref[...] + b_ref[...]


def _proj_ln(a, w, h, g, b):
    t, k = a.shape
    d = w.shape[1]
    tm = ROW_TILE
    vec = pl.BlockSpec((1, d), lambda i: (0, 0))
    return pl.pallas_call(
        _proj_ln_kernel,
        grid=(t // tm,),
        in_specs=[pl.BlockSpec((tm, k), lambda i: (i, 0)), pl.BlockSpec((k, d), lambda i: (0, 0)),
                  pl.BlockSpec((tm, d), lambda i: (i, 0)), vec, vec],
        out_specs=pl.BlockSpec((tm, d), lambda i: (i, 0)),
        out_shape=jax.ShapeDtypeStruct((t, d), F32),
        compiler_params=pltpu.CompilerParams(dimension_semantics=("arbitrary",),
                                             vmem_limit_bytes=VMEM_LIMIT_BYTES),
        name="proj_ln",
    )(a, w.astype(BF16), h, g.reshape(1, d), b.reshape(1, d))


RET_HEADS = 4
RET_DK = 256
RET_DV = 512
RET_HK = RET_HEADS * RET_DK
RET_HV = RET_HEADS * RET_DV
RET_CHUNK = 128
ROPE_BASE = 10000.0
RET_COL_TILE = 256


def _ret_in_kernel(x_ref, pos_ref, freq_ref, w_ref, qkv_ref, gate_ref, cos_ref, sin_ref):
    j = pl.program_id(1)
    n_qk = 2 * RET_HK // RET_COL_TILE
    n_qkv = n_qk + RET_HV // RET_COL_TILE

    @pl.when(j == 0)
    def _():
        ang = pos_ref[...].astype(F32) * freq_ref[...]
        cos_ref[...] = jnp.cos(ang)
        sin_ref[...] = jnp.sin(ang)

    y = jnp.dot(x_ref[...].astype(BF16), w_ref[...], preferred_element_type=F32)

    @pl.when(j < n_qk)
    def _():
        half = RET_DK // 2
        x1, x2 = y[:, :half], y[:, half:]
        c, s = cos_ref[...], sin_ref[...]
        scale = jnp.where(j < n_qk // 2, 1.0, RET_DK ** -0.5)
        qkv_ref[:, :half] = ((x1 * c - x2 * s) * scale).astype(BF16)
        qkv_ref[:, half:] = ((x1 * s + x2 * c) * scale).astype(BF16)

    @pl.when((j >= n_qk) & (j < n_qkv))
    def _():
        qkv_ref[...] = y.astype(BF16)

    @pl.when(j >= n_qkv)
    def _():
        gate_ref[...] = y


def _ret_in(x, pos, w_in):
    t, d = x.shape
    tm, tn = ROW_TILE, RET_COL_TILE
    n_qkv = (2 * RET_HK + RET_HV) // tn
    n_all = w_in.shape[1] // tn
    half = RET_DK // 2
    freq = (ROPE_BASE ** (-jnp.arange(0, RET_DK, 2, dtype=F32) / RET_DK)).reshape(1, half)
    return pl.pallas_call(
        _ret_in_kernel,
        grid=(t // tm, n_all),
        in_specs=[pl.BlockSpec((tm, d), lambda i, j: (i, 0)),
                  pl.BlockSpec((tm, 1), lambda i, j: (i, 0)),
                  pl.BlockSpec((1, half), lambda i, j: (0, 0)),
                  pl.BlockSpec((d, tn), lambda i, j: (0, j))],
        out_specs=[pl.BlockSpec((tm, tn), lambda i, j: (i, jnp.minimum(j, n_qkv - 1))),
                   pl.BlockSpec((tm, tn), lambda i, j: (i, jnp.maximum(j - n_qkv, 0)))],
        out_shape=[jax.ShapeDtypeStruct((t, n_qkv * tn), BF16),
                   jax.ShapeDtypeStruct((t, (n_all - n_qkv) * tn), F32)],
        scratch_shapes=[pltpu.VMEM((tm, half), F32), pltpu.VMEM((tm, half), F32)],
        compiler_params=pltpu.CompilerParams(dimension_semantics=("arbitrary", "arbitrary"),
                                             vmem_limit_bytes=VMEM_LIMIT_BYTES),
        name="ret_in",
    )(x, pos.reshape(t, 1), freq, w_in.astype(BF16))


def _ret_scan_kernel(q_ref, k_ref, v_ref, gate_ref, dec_ref, xi_ref, zeta_ref, gc_ref, gng_ref, gnb_ref,
                     o_ref, state_ref, y_ref):
    c = RET_CHUNK
    n_chunks = q_ref.shape[0] // c

    def chunk_out(i, d):
        rows = pl.ds(pl.multiple_of(i * c, c), c)
        qi, ki, vi = q_ref[rows, :], k_ref[rows, :], v_ref[rows, :]
        sc = lax.dot_general(qi, ki, (((1,), (1,)), ((), ())), preferred_element_type=F32)
        inner = jnp.dot((sc * dec_ref[d]).astype(BF16), vi, preferred_element_type=F32)
        cross = jnp.dot(qi, state_ref[...].astype(BF16), preferred_element_type=F32) * xi_ref[d]
        kz = (ki.astype(F32) * zeta_ref[d]).astype(BF16)
        upd = lax.dot_general(kz, vi, (((0,), (0,)), ((), ())), preferred_element_type=F32)
        state_ref[...] = gc_ref[d] * state_ref[...] + upd
        return rows, inner + cross

    state_ref[...] = jnp.zeros_like(state_ref)

    def fwd(i, carry):
        rows, y = chunk_out(i, 0)
        y_ref[rows, :] = y
        return carry

    lax.fori_loop(0, n_chunks, fwd, 0)
    state_ref[...] = jnp.zeros_like(state_ref)

    def bwd(n, carry):
        rows, y = chunk_out(n_chunks - 1 - n, 1)
        y = y + y_ref[rows, :]
        mu = jnp.mean(y, axis=-1, keepdims=True)
        dlt = y - mu
        var = jnp.mean(dlt * dlt, axis=-1, keepdims=True)
        yn = dlt * lax.rsqrt(var + LN_EPS) * gng_ref[...] + gnb_ref[...]
        g = gate_ref[rows, :]
        o_ref[rows, :] = (g * (1.0 / (1.0 + jnp.exp(-g))) * yn).astype(BF16)
        return carry

    lax.fori_loop(0, n_chunks, bwd, 0)


def _ret_scan(qkv, gate, log1m_decay, gn_g, gn_b, batch, seq):
    c = RET_CHUNK
    log_gamma = jnp.log1p(-jnp.exp(log1m_decay.astype(F32)))
    idx = jnp.arange(c, dtype=F32)
    diff = idx[:, None] - idx[None, :]
    lg = log_gamma[:, :, None, None]
    dec_f = jnp.where(diff >= 0, jnp.exp(jnp.maximum(diff, 0.0) * lg[0]), 0.0)
    dec_b = jnp.where(diff < 0, jnp.exp(jnp.maximum(-diff, 0.0) * lg[1]), 0.0)
    dec = jnp.stack([dec_f, dec_b], axis=1)
    lgc = log_gamma[:, :, None]
    xi = jnp.stack([jnp.exp((idx + 1.0) * lgc[0]), jnp.exp((c - idx) * lgc[1])], axis=1)[..., None]
    zeta = jnp.stack([jnp.exp((c - 1.0 - idx) * lgc[0]), jnp.exp(idx * lgc[1])], axis=1)[..., None]
    gc = jnp.exp(c * log_gamma).T.reshape(RET_HEADS, 2, 1, 1)
    t = batch * seq
    kq = RET_HK // RET_DK
    kv = 2 * RET_HK // RET_DV
    head4 = lambda b, h: (h, 0, 0, 0)
    return pl.pallas_call(
        _ret_scan_kernel,
        grid=(batch, RET_HEADS),
        in_specs=[pl.BlockSpec((seq, RET_DK), lambda b, h: (b, h)),
                  pl.BlockSpec((seq, RET_DK), lambda b, h: (b, kq + h)),
                  pl.BlockSpec((seq, RET_DV), lambda b, h: (b, kv + h)),
                  pl.BlockSpec((seq, RET_DV), lambda b, h: (b, h)),
                  pl.BlockSpec((None, 2, c, c), head4),
                  pl.BlockSpec((None, 2, c, 1), head4),
                  pl.BlockSpec((None, 2, c, 1), head4),
                  pl.BlockSpec((None, 2, 1, 1), head4),
                  pl.BlockSpec((1, RET_DV), lambda b, h: (0, h)),
                  pl.BlockSpec((1, RET_DV), lambda b, h: (0, h))],
        out_specs=pl.BlockSpec((seq, RET_DV), lambda b, h: (b, h)),
        out_shape=jax.ShapeDtypeStruct((t, RET_HV), BF16),
        scratch_shapes=[pltpu.VMEM((RET_DK, RET_DV), F32), pltpu.VMEM((seq, RET_DV), F32)],
        compiler_params=pltpu.CompilerParams(dimension_semantics=("arbitrary", "arbitrary"),
                                             vmem_limit_bytes=VMEM_LIMIT_BYTES),
        name="ret_scan",
    )(qkv, qkv, qkv, gate, dec, xi, zeta, gc, gn_g.reshape(1, RET_HV), gn_b.reshape(1, RET_HV))


def _retention_block(h, pos, w_in, log1m_decay, gn_g, gn_b, w_out, ln_g, ln_b, batch, seq):
    qkv, gate = _ret_in(h, pos, w_in)
    gated = _ret_scan(qkv, gate, log1m_decay, gn_g, gn_b, batch, seq)
    return _proj_ln(gated, w_out, h, ln_g, ln_b)


MLA_HEADS = 8
MLA_NOPE = 128
MLA_ROPE = 64
MLA_VDIM = 128
MLA_Q_RANK = 384
MLA_KV_RANK = 256
MLA_QK_PAD = 256
MLA_Q_TILE = 256


def _rms(x, g):
    ms = jnp.mean(x * x, axis=-1, keepdims=True)
    return x * lax.rsqrt(ms + LN_EPS) * g


def _mla_in_kernel(x_ref, pos_ref, freq_ref, win_ref, qg_ref, kvg_ref, wuq_ref, wukv_ref, q_ref, k_ref, v_ref):
    c = jnp.dot(x_ref[...].astype(BF16), win_ref[...], preferred_element_type=F32)
    lane = lax.broadcasted_iota(I32, (1, LANES), 1)
    half = MLA_ROPE // 2
    ang = pos_ref[...].astype(F32) * freq_ref[...]
    cosv = jnp.where(lane < MLA_ROPE, jnp.cos(ang), 0.0)
    sinv = jnp.sin(ang)
    sinv = jnp.where(lane < half, -sinv, jnp.where(lane < MLA_ROPE, sinv, 0.0))

    def rope(y):
        swapped = pltpu.roll(y, half, 1) + pltpu.roll(y, LANES - half, 1)
        return y * cosv + swapped * sinv

    cq = _rms(c[:, :MLA_Q_RANK], qg_ref[...]).astype(BF16)
    ckv = _rms(c[:, MLA_Q_RANK:MLA_Q_RANK + MLA_KV_RANK], kvg_ref[...]).astype(BF16)
    k_rope = rope(c[:, MLA_Q_RANK + MLA_KV_RANK:]).astype(BF16)
    q = jnp.dot(cq, wuq_ref[...], preferred_element_type=F32)
    kv = jnp.dot(ckv, wukv_ref[...], preferred_element_type=F32)
    for hd in range(MLA_HEADS):
        o = hd * MLA_QK_PAD
        q_ref[:, o:o + MLA_NOPE] = q[:, o:o + MLA_NOPE].astype(BF16)
        q_ref[:, o + MLA_NOPE:o + MLA_QK_PAD] = rope(q[:, o + MLA_NOPE:o + MLA_QK_PAD]).astype(BF16)
        k_ref[:, o:o + MLA_NOPE] = kv[:, o:o + MLA_NOPE].astype(BF16)
        k_ref[:, o + MLA_NOPE:o + MLA_QK_PAD] = k_rope
        v_ref[:, hd * MLA_VDIM:(hd + 1) * MLA_VDIM] = kv[:, o + MLA_NOPE:o + MLA_QK_PAD].astype(BF16)


def _mla_in(x, pos, w_in, q_norm_g, kv_norm_g, w_uq, w_ukv):
    t, d = x.shape
    tm = ROW_TILE
    pad = MLA_QK_PAD - MLA_NOPE - MLA_ROPE
    win = jnp.pad(w_in, ((0, 0), (0, pad))).astype(BF16)
    wuq = jnp.pad(w_uq.reshape(MLA_Q_RANK, MLA_HEADS, MLA_NOPE + MLA_ROPE), ((0, 0), (0, 0), (0, pad)))
    wuq = wuq.reshape(MLA_Q_RANK, MLA_HEADS * MLA_QK_PAD).astype(BF16)
    wukv = w_ukv.astype(BF16)
    half = MLA_ROPE // 2
    f = ROPE_BASE ** (-jnp.arange(0, MLA_ROPE, 2, dtype=F32) / MLA_ROPE)
    freq = jnp.concatenate([f, f, jnp.zeros((LANES - MLA_ROPE,), F32)]).reshape(1, LANES)
    full = lambda a: pl.BlockSpec(a.shape, lambda i: (0,) * a.ndim)
    qg = q_norm_g.reshape(1, -1)
    kvg = kv_norm_g.reshape(1, -1)
    wide = MLA_HEADS * MLA_QK_PAD
    return pl.pallas_call(
        _mla_in_kernel,
        grid=(t // tm,),
        in_specs=[pl.BlockSpec((tm, d), lambda i: (i, 0)), pl.BlockSpec((tm, 1), lambda i: (i, 0)),
                  full(freq), full(win), full(qg), full(kvg), full(wuq), full(wukv)],
        out_specs=[pl.BlockSpec((tm, wide), lambda i: (i, 0)), pl.BlockSpec((tm, wide), lambda i: (i, 0)),
                   pl.BlockSpec((tm, MLA_HEADS * MLA_VDIM), lambda i: (i, 0))],
        out_shape=[jax.ShapeDtypeStruct((t, wide), BF16), jax.ShapeDtypeStruct((t, wide), BF16),
                   jax.ShapeDtypeStruct((t, MLA_HEADS * MLA_VDIM), BF16)],
        compiler_params=pltpu.CompilerParams(dimension_semantics=("arbitrary",),
                                             vmem_limit_bytes=VMEM_LIMIT_BYTES),
        name="mla_in",
    )(x, pos.reshape(t, 1), freq, win, qg, kvg, wuq, wukv)


def _mla_attn_kernel(q_ref, k_ref, v_ref, o_ref):
    s = lax.dot_general(q_ref[...], k_ref[...], (((1,), (1,)), ((), ())), preferred_element_type=F32)
    s = s * ((MLA_NOPE + MLA_ROPE) ** -0.5)
    p = jnp.exp(s - jnp.max(s, axis=-1, keepdims=True))
    denom = jnp.sum(p, axis=-1, keepdims=True)
    o = jnp.dot(p.astype(BF16), v_ref[...], preferred_element_type=F32)
    o_ref[...] = (o / denom).astype(BF16)


def _mla_attn(q, k, v, batch, seq):
    t = batch * seq
    tq = MLA_Q_TILE
    nq = seq // tq
    return pl.pallas_call(
        _mla_attn_kernel,
        grid=(batch, MLA_HEADS, nq),
        in_specs=[pl.BlockSpec((tq, MLA_QK_PAD), lambda b, h, i: (b * nq + i, h)),
                  pl.BlockSpec((seq, MLA_QK_PAD), lambda b, h, i: (b, h)),
                  pl.BlockSpec((seq, MLA_VDIM), lambda b, h, i: (b, h))],
        out_specs=pl.BlockSpec((tq, MLA_VDIM), lambda b, h, i: (b * nq + i, h)),
        out_shape=jax.ShapeDtypeStruct((t, MLA_HEADS * MLA_VDIM), BF16),
        compiler_params=pltpu.CompilerParams(dimension_semantics=("arbitrary", "arbitrary", "arbitrary"),
                                             vmem_limit_bytes=VMEM_LIMIT_BYTES),
        name="mla_attn",
    )(q, k, v)


def _mla_block(h, pos, w_in, q_norm_g, kv_norm_g, w_uq, w_ukv, w_out, ln_g, ln_b, batch, seq):
    q, k, v = _mla_in(h, pos, w_in, q_norm_g, kv_norm_g, w_uq, w_ukv)
    o = _mla_attn(q, k, v, batch, seq)
    return _proj_ln(o, w_out, h, ln_g, ln_b)


def kernel(x, positions, ret_w_in, ret_log1m_decay, ret_gn_g, ret_gn_b, ret_w_out, mla_w_in, mla_q_norm,
           mla_kv_norm, mla_w_uq, mla_w_ukv, mla_w_out, peer_w_q, peer_sub_keys, peer_u, peer_v, ln_mix_g,
           ln_mix_b, ln_ffn_g, ln_ffn_b):
    b, s, d = x.shape
    t = b * s
    h = x.reshape(t, d)
    pos = positions.reshape(t)
    for i in range(DEPTH):
        j = i // 2
        if i % 2 == 0:
            h = _retention_block(h, pos, ret_w_in[j], ret_log1m_decay[j], ret_gn_g[j], ret_gn_b[j],
                                 ret_w_out[j], ln_mix_g[i], ln_mix_b[i], b, s)
        else:
            h = _mla_block(h, pos, mla_w_in[j], mla_q_norm[j], mla_kv_norm[j], mla_w_uq[j], mla_w_ukv[j],
                           mla_w_out[j], ln_mix_g[i], ln_mix_b[i], b, s)
        ffn = _peer_ffn(h, peer_w_q[i], peer_sub_keys[i], peer_u[i], peer_v[i])
        h = _add_ln(h, ffn, ln_ffn_g[i], ln_ffn_b[i])
    return h.reshape(b, s, d)
```

```python
import functools

import jax
import jax.numpy as jnp
from jax import lax
from jax.experimental import pallas as pl
from jax.experimental.pallas import tpu as pltpu

F32 = jnp.float32
BF16 = jnp.bfloat16
I32 = jnp.int32
U32 = jnp.uint32

D_MODEL = 1024
DEPTH = 2
DN_ALPHA = (2.0 * DEPTH) ** 0.25
LN_EPS = 1e-5

PEER_HEADS = 8
PEER_NKEYS = 128
PEER_HALF = 128
PEER_TOPK = 16
PEER_SLOTS = PEER_HEADS * PEER_TOPK
PEER_EXPERTS = PEER_NKEYS * PEER_NKEYS
PEER_HALF_EXPERTS = PEER_EXPERTS // 2

SUBLANES = 8
LANES = 128
VMEM_LIMIT_BYTES = 56 * 1024 * 1024

SEL_TOKENS = 256
GATHER_TOKENS = 64


def _top_rows(st, ids, k, payload=None):
    vals, sel_ids, sel_pay = [], [], []
    big = jnp.int32(2 ** 30)
    for _ in range(k):
        m = jnp.max(st, axis=0, keepdims=True)
        sel = jnp.min(jnp.where(st == m, ids, big), axis=0, keepdims=True)
        hit = ids == sel
        vals.append(m)
        sel_ids.append(sel)
        if payload is not None:
            sel_pay.append(jnp.max(jnp.where(hit, payload, -1), axis=0, keepdims=True))
        st = jnp.where(hit, -jnp.inf, st)
    return vals, sel_ids, sel_pay


def _peer_select_kernel(h_ref, wq_ref, keys_ref, eidx_ref, gate_ref, s_buf, i_buf):
    tb = h_ref.shape[0]
    q = jnp.dot(h_ref[...].astype(BF16), wq_ref[...], preferred_element_type=F32)
    key_ids = lax.broadcasted_iota(I32, (PEER_NKEYS, tb), 0)
    for head in range(PEER_HEADS):
        for c in range(2):
            col = (head * 2 + c) * PEER_HALF
            qhc = q[:, col:col + PEER_HALF].astype(BF16)
            st = lax.dot_general(keys_ref[head, c], qhc, (((1,), (1,)), ((), ())),
                                 preferred_element_type=F32)
            vals, ids, _ = _top_rows(st, key_ids, PEER_TOPK)
            for a in range(PEER_TOPK):
                s_buf[c, pl.ds(a, 1), :] = vals[a]
                i_buf[c, pl.ds(a, 1), :] = ids[a]
        cand_s, cand_pos, cand_e = [], [], []
        for b in range(PEER_TOPK):
            na = PEER_TOPK if b == 0 else SUBLANES
            cand_s.append(s_buf[0, :na, :] + s_buf[1, pl.ds(b, 1), :])
            cand_pos.append(lax.broadcasted_iota(I32, (na, tb), 0) * PEER_TOPK + b)
            cand_e.append(i_buf[0, :na, :] * PEER_NKEYS + i_buf[1, pl.ds(b, 1), :])
        cand_s = jnp.concatenate(cand_s, axis=0)
        cand_pos = jnp.concatenate(cand_pos, axis=0)
        cand_e = jnp.concatenate(cand_e, axis=0)
        top_s, _, top_e = _top_rows(cand_s, cand_pos, PEER_TOPK, payload=cand_e)
        ex = [jnp.exp(t - top_s[0]) for t in top_s]
        denom = functools.reduce(lambda x, y: x + y, ex)
        inv = 1.0 / denom
        for k in range(PEER_TOPK):
            slot = head * PEER_TOPK + k
            eidx_ref[pl.ds(slot, 1), :] = top_e[k]
            gate_ref[pl.ds(slot, 1), :] = ex[k] * inv


def _peer_select(h, w_q, sub_keys):
    t = h.shape[0]
    tb = SEL_TOKENS
    wq = w_q.astype(BF16)
    keys = sub_keys.astype(BF16)
    eidx_t, gate_t = pl.pallas_call(
        _peer_select_kernel,
        grid=(t // tb,),
        in_specs=[pl.BlockSpec((tb, D_MODEL), lambda i: (i, 0)),
                  pl.BlockSpec(wq.shape, lambda i: (0, 0)),
                  pl.BlockSpec(keys.shape, lambda i: (0, 0, 0, 0))],
        out_specs=[pl.BlockSpec((PEER_SLOTS, tb), lambda i: (0, i)),
                   pl.BlockSpec((PEER_SLOTS, tb), lambda i: (0, i))],
        out_shape=[jax.ShapeDtypeStruct((PEER_SLOTS, t), I32),
                   jax.ShapeDtypeStruct((PEER_SLOTS, t), F32)],
        scratch_shapes=[pltpu.VMEM((2, PEER_TOPK, tb), F32), pltpu.VMEM((2, PEER_TOPK, tb), I32)],
        compiler_params=pltpu.CompilerParams(dimension_semantics=("arbitrary",),
                                             vmem_limit_bytes=VMEM_LIMIT_BYTES),
        name="peer_select",
    )(h, wq, keys)
    return eidx_t.T, gate_t.T


def _pack_table(w):
    bits = lax.bitcast_convert_type(w.astype(BF16), jnp.uint16).astype(U32)
    packed = bits[:PEER_HALF_EXPERTS] | (bits[PEER_HALF_EXPERTS:] << 16)
    return packed.reshape(PEER_HALF_EXPERTS * SUBLANES, LANES)


HIGH_HALF = 0xFFFF0000


def _packed_rowsum(ws, sub):
    def add(a, b):
        return pltpu.bitcast(pltpu.bitcast(a, BF16) + pltpu.bitcast(b, BF16), U32)

    m4 = sub < 4
    lvl1 = []
    for j in range(4):
        x, y = ws[j], ws[j + 4]
        lvl1.append(add(jnp.where(m4, x, y), pltpu.roll(jnp.where(m4, y, x), 4, 0)))
    m2 = (sub & 2) == 0
    lvl2 = []
    for j in range(2):
        x, y = lvl1[j], lvl1[j + 2]
        lvl2.append(add(jnp.where(m2, x, pltpu.roll(y, 2, 0)), jnp.where(m2, pltpu.roll(x, 6, 0), y)))
    m1 = (sub & 1) == 0
    x, y = lvl2[0], lvl2[1]
    return add(jnp.where(m1, x, pltpu.roll(y, 1, 0)), jnp.where(m1, pltpu.roll(x, 7, 0), y))


def _peer_u_kernel(row_ref, x_ref, up_ref, gate_ref, tbl_ref, c_ref):
    tt = x_ref.shape[0]
    n_chain = 2
    sub = lax.broadcasted_iota(I32, (SUBLANES, LANES), 0)
    lane = lax.broadcasted_iota(I32, (SUBLANES, LANES), 1)
    high = jnp.uint32(HIGH_HALF)

    def group(g, carry):
        t0 = pl.multiple_of(g * SUBLANES, SUBLANES)
        xs = []
        for j in range(SUBLANES):
            bits = pltpu.bitcast(x_ref[t0 + j].astype(BF16).astype(F32), U32)
            xs.append(pltpu.bitcast(bits | (bits >> 16), BF16))
        base = g * (PEER_SLOTS * SUBLANES)
        lo = [jnp.zeros((SUBLANES, LANES), F32) for _ in range(n_chain)]
        hi = [jnp.zeros((SUBLANES, LANES), F32) for _ in range(n_chain)]
        for p in range(PEER_SLOTS):
            ws = []
            for j in range(SUBLANES):
                row8 = pl.multiple_of(row_ref[base + (p * SUBLANES + j)], SUBLANES)
                w = pltpu.bitcast(tbl_ref[pl.ds(row8, SUBLANES), :], BF16)
                ws.append(pltpu.bitcast(w * xs[j], U32))
            r = _packed_rowsum(ws, sub)
            here = lane == p
            c = p % n_chain
            lo[c] = jnp.where(here, jnp.sum(pltpu.bitcast(r << 16, F32), axis=1, keepdims=True), lo[c])
            hi[c] = jnp.where(here, jnp.sum(pltpu.bitcast(r & high, F32), axis=1, keepdims=True), hi[c])
        rows = pl.ds(t0, SUBLANES)
        hdot = jnp.where(up_ref[rows, :] > 0.5, hi[0] + hi[1], lo[0] + lo[1])
        act = 0.5 * hdot * (1.0 + lax.erf(hdot * (2.0 ** -0.5)))
        c_ref[rows, :] = act * gate_ref[rows, :]
        return carry

    lax.fori_loop(0, tt // SUBLANES, group, 0)


def _peer_v_kernel(code_ref, tbl_ref, o_ref):
    tt = o_ref.shape[0]
    n_acc = 4

    def token(t, carry):
        base = t * PEER_SLOTS
        accs = [jnp.zeros((SUBLANES, LANES), F32) for _ in range(n_acc)]
        for p in range(PEER_SLOTS):
            code = code_ref[base + p]
            row8 = pl.multiple_of(code & 0xFFF8, SUBLANES)
            cv = jnp.full((SUBLANES, LANES), code, I32)
            shift = (cv & 4) << 2
            coef = pltpu.bitcast(cv & jnp.int32(-65536), F32)
            w = pltpu.bitcast(tbl_ref[pl.ds(row8, SUBLANES), :], I32)
            f = pltpu.bitcast((w << shift) & jnp.int32(-65536), F32)
            accs[p % n_acc] = accs[p % n_acc] + coef * f
        o_ref[t] = (accs[0] + accs[1]) + (accs[2] + accs[3])
        return carry

    lax.fori_loop(0, tt, token, 0)


def _peer_retrieve(h, eidx, gate, u_tbl, v_tbl):
    t = h.shape[0]
    tt = GATHER_TOKENS
    row8 = (eidx & (PEER_HALF_EXPERTS - 1)) * SUBLANES
    upper = (eidx >= PEER_HALF_EXPERTS).astype(I32)

    def grouped(a):
        return a.reshape(t // SUBLANES, SUBLANES, PEER_SLOTS).transpose(0, 2, 1).reshape(-1)

    x3 = h.reshape(t, SUBLANES, LANES)
    smem_spec = pl.BlockSpec((tt * PEER_SLOTS,), lambda i: (i,), memory_space=pltpu.SMEM)
    vmem_slots = pl.BlockSpec((tt, PEER_SLOTS), lambda i: (i, 0))
    tile3 = pl.BlockSpec((tt, SUBLANES, LANES), lambda i: (i, 0, 0))
    table_spec = pl.BlockSpec(memory_space=pltpu.VMEM)
    params = pltpu.CompilerParams(dimension_semantics=("arbitrary",), vmem_limit_bytes=VMEM_LIMIT_BYTES)
    coef = pl.pallas_call(
        _peer_u_kernel,
        grid=(t // tt,),
        in_specs=[smem_spec, tile3, vmem_slots, vmem_slots, table_spec],
        out_specs=vmem_slots,
        out_shape=jax.ShapeDtypeStruct((t, PEER_SLOTS), F32),
        compiler_params=params,
        name="peer_u",
    )(grouped(row8), x3, upper.astype(F32), gate, u_tbl)
    coef_bits = lax.bitcast_convert_type(coef.astype(BF16), jnp.uint16).astype(I32) << 16
    code = (coef_bits | row8 | (4 - 4 * upper)).reshape(-1)
    out3 = pl.pallas_call(
        _peer_v_kernel,
        grid=(t // tt,),
        in_specs=[smem_spec, table_spec],
        out_specs=tile3,
        out_shape=jax.ShapeDtypeStruct((t, SUBLANES, LANES), F32),
        compiler_params=params,
        name="peer_v",
    )(code, v_tbl)
    return out3.reshape(t, D_MODEL)


def _add_ln_kernel(h_ref, f_ref, g_ref, b_ref, o_ref):
    y = DN_ALPHA * h_ref[...] + f_ref[...]
    mu = jnp.mean(y, axis=-1, keepdims=True)
    d = y - mu
    var = jnp.mean(d * d, axis=-1, keepdims=True)
    o_ref[...] = d * lax.rsqrt(var + LN_EPS) * g_ref[...] + b_ref[...]


def _add_ln(h, f, g, b):
    t, d = h.shape
    tb = 512
    row = pl.BlockSpec((tb, d), lambda i: (i, 0))
    vec = pl.BlockSpec((1, d), lambda i: (0, 0))
    return pl.pallas_call(
        _add_ln_kernel,
        grid=(t // tb,),
        in_specs=[row, row, vec, vec],
        out_specs=row,
        out_shape=jax.ShapeDtypeStruct((t, d), F32),
        compiler_params=pltpu.CompilerParams(dimension_semantics=("arbitrary",)),
        name="add_ln",
    )(h, f, g.reshape(1, d), b.reshape(1, d))


def _peer_ffn(h, w_q, sub_keys, u, v):
    eidx, gate = _peer_select(h, w_q, sub_keys)
    return _peer_retrieve(h, eidx, gate, _pack_table(u), _pack_table(v))


ROW_TILE = 512


def _proj_ln_kernel(a_ref, w_ref, h_ref, g_ref, b_ref, o_ref):
    y = DN_ALPHA * h_ref[...] + jnp.dot(a_ref[...], w_ref[...], preferred_element_type=F32)
    mu = jnp.mean(y, axis=-1, keepdims=True)
    d = y - mu
    var = jnp.mean(d * d, axis=-1, keepdims=True)
    o_ref[...] = d * lax.rsqrt(var + LN_EPS) * g_ref[...] + b_ref[...]


def _proj_ln(a, w, h, g, b):
    t, k = a.shape
    d = w.shape[1]
    tm = ROW_TILE
    vec = pl.BlockSpec((1, d), lambda i: (0, 0))
    return pl.pallas_call(
        _proj_ln_kernel,
        grid=(t // tm,),
        in_specs=[pl.BlockSpec((tm, k), lambda i: (i, 0)), pl.BlockSpec((k, d), lambda i: (0, 0)),
                  pl.BlockSpec((tm, d), lambda i: (i, 0)), vec, vec],
        out_specs=pl.BlockSpec((tm, d), lambda i: (i, 0)),
        out_shape=jax.ShapeDtypeStruct((t, d), F32),
        compiler_params=pltpu.CompilerParams(dimension_semantics=("arbitrary",),
                                             vmem_limit_bytes=VMEM_LIMIT_BYTES),
        name="proj_ln",
    )(a, w.astype(BF16), h, g.reshape(1, d), b.reshape(1, d))


RET_HEADS = 4
RET_DK = 256
RET_DV = 512
RET_HK = RET_HEADS * RET_DK
RET_HV = RET_HEADS * RET_DV
RET_CHUNK = 128
ROPE_BASE = 10000.0
RET_COL_TILE = 256


def _ret_in_kernel(x_ref, pos_ref, freq_ref, w_ref, qkv_ref, gate_ref):
    tn = RET_COL_TILE
    n_qk = 2 * RET_HK // tn
    n_qkv = n_qk + RET_HV // tn
    half = RET_DK // 2
    x = x_ref[...].astype(BF16)
    ang = pos_ref[...].astype(F32) * freq_ref[...]
    c, s = jnp.cos(ang), jnp.sin(ang)
    for j in range(w_ref.shape[1] // tn):
        y = jnp.dot(x, w_ref[:, j * tn:(j + 1) * tn], preferred_element_type=F32)
        if j < n_qk:
            x1, x2 = y[:, :half], y[:, half:]
            scale = 1.0 if j < n_qk // 2 else RET_DK ** -0.5
            qkv_ref[:, j * tn:j * tn + half] = ((x1 * c - x2 * s) * scale).astype(BF16)
            qkv_ref[:, j * tn + half:(j + 1) * tn] = ((x1 * s + x2 * c) * scale).astype(BF16)
        elif j < n_qkv:
            qkv_ref[:, j * tn:(j + 1) * tn] = y.astype(BF16)
        else:
            gate_ref[:, (j - n_qkv) * tn:(j - n_qkv + 1) * tn] = y


def _ret_in(x, pos, w_in):
    t, d = x.shape
    tm = ROW_TILE
    n_all = w_in.shape[1]
    n_qkv = 2 * RET_HK + RET_HV
    half = RET_DK // 2
    freq = (ROPE_BASE ** (-jnp.arange(0, RET_DK, 2, dtype=F32) / RET_DK)).reshape(1, half)
    return pl.pallas_call(
        _ret_in_kernel,
        grid=(t // tm,),
        in_specs=[pl.BlockSpec((tm, d), lambda i: (i, 0)),
                  pl.BlockSpec((tm, 1), lambda i: (i, 0)),
                  pl.BlockSpec((1, half), lambda i: (0, 0)),
                  pl.BlockSpec((d, n_all), lambda i: (0, 0))],
        out_specs=[pl.BlockSpec((tm, n_qkv), lambda i: (i, 0)),
                   pl.BlockSpec((tm, n_all - n_qkv), lambda i: (i, 0))],
        out_shape=[jax.ShapeDtypeStruct((t, n_qkv), BF16),
                   jax.ShapeDtypeStruct((t, n_all - n_qkv), F32)],
        compiler_params=pltpu.CompilerParams(dimension_semantics=("arbitrary",),
                                             vmem_limit_bytes=VMEM_LIMIT_BYTES),
        name="ret_in",
    )(x, pos.reshape(t, 1), freq, w_in.astype(BF16))


def _ret_scan_kernel(q_ref, k_ref, v_ref, gate_ref, dec_ref, xi_ref, zeta_ref, gc_ref, gng_ref, gnb_ref,
                     o_ref, state_ref, y_ref):
    c = RET_CHUNK
    n_chunks = q_ref.shape[0] // c

    def chunk_out(i, d):
        rows = pl.ds(pl.multiple_of(i * c, c), c)
        qi, ki, vi = q_ref[rows, :], k_ref[rows, :], v_ref[rows, :]
        sc = lax.dot_general(qi, ki, (((1,), (1,)), ((), ())), preferred_element_type=F32)
        inner = jnp.dot((sc * dec_ref[d]).astype(BF16), vi, preferred_element_type=F32)
        cross = jnp.dot(qi, state_ref[...].astype(BF16), preferred_element_type=F32) * xi_ref[d]
        kz = (ki.astype(F32) * zeta_ref[d]).astype(BF16)
        upd = lax.dot_general(kz, vi, (((0,), (0,)), ((), ())), preferred_element_type=F32)
        state_ref[...] = gc_ref[d] * state_ref[...] + upd
        return rows, inner + cross

    state_ref[...] = jnp.zeros_like(state_ref)

    def fwd(i, carry):
        rows, y = chunk_out(i, 0)
        y_ref[rows, :] = y
        return carry

    lax.fori_loop(0, n_chunks, fwd, 0)
    state_ref[...] = jnp.zeros_like(state_ref)

    def bwd(n, carry):
        rows, y = chunk_out(n_chunks - 1 - n, 1)
        y = y + y_ref[rows, :]
        mu = jnp.mean(y, axis=-1, keepdims=True)
        dlt = y - mu
        var = jnp.mean(dlt * dlt, axis=-1, keepdims=True)
        yn = dlt * lax.rsqrt(var + LN_EPS) * gng_ref[...] + gnb_ref[...]
        g = gate_ref[rows, :]
        o_ref[rows, :] = (g * (1.0 / (1.0 + jnp.exp(-g))) * yn).astype(BF16)
        return carry

    lax.fori_loop(0, n_chunks, bwd, 0)


def _ret_scan(qkv, gate, log1m_decay, gn_g, gn_b, batch, seq):
    c = RET_CHUNK
    log_gamma = jnp.log1p(-jnp.exp(log1m_decay.astype(F32)))
    idx = jnp.arange(c, dtype=F32)
    diff = idx[:, None] - idx[None, :]
    lg = log_gamma[:, :, None, None]
    dec_f = jnp.where(diff >= 0, jnp.exp(jnp.maximum(diff, 0.0) * lg[0]), 0.0)
    dec_b = jnp.where(diff < 0, jnp.exp(jnp.maximum(-diff, 0.0) * lg[1]), 0.0)
    dec = jnp.stack([dec_f, dec_b], axis=1)
    lgc = log_gamma[:, :, None]
    xi = jnp.stack([jnp.exp((idx + 1.0) * lgc[0]), jnp.exp((c - idx) * lgc[1])], axis=1)[..., None]
    zeta = jnp.stack([jnp.exp((c - 1.0 - idx) * lgc[0]), jnp.exp(idx * lgc[1])], axis=1)[..., None]
    gc = jnp.exp(c * log_gamma).T.reshape(RET_HEADS, 2, 1, 1)
    t = batch * seq
    kq = RET_HK // RET_DK
    kv = 2 * RET_HK // RET_DV
    head4 = lambda b, h: (h, 0, 0, 0)
    return pl.pallas_call(
        _ret_scan_kernel,
        grid=(batch, RET_HEADS),
        in_specs=[pl.BlockSpec((seq, RET_DK), lambda b, h: (b, h)),
                  pl.BlockSpec((seq, RET_DK), lambda b, h: (b, kq + h)),
                  pl.BlockSpec((seq, RET_DV), lambda b, h: (b, kv + h)),
                  pl.BlockSpec((seq, RET_DV), lambda b, h: (b, h)),
                  pl.BlockSpec((None, 2, c, c), head4),
                  pl.BlockSpec((None, 2, c, 1), head4),
                  pl.BlockSpec((None, 2, c, 1), head4),
                  pl.BlockSpec((None, 2, 1, 1), head4),
                  pl.BlockSpec((1, RET_DV), lambda b, h: (0, h)),
                  pl.BlockSpec((1, RET_DV), lambda b, h: (0, h))],
        out_specs=pl.BlockSpec((seq, RET_DV), lambda b, h: (b, h)),
        out_shape=jax.ShapeDtypeStruct((t, RET_HV), BF16),
        scratch_shapes=[pltpu.VMEM((RET_DK, RET_DV), F32), pltpu.VMEM((seq, RET_DV), F32)],
        compiler_params=pltpu.CompilerParams(dimension_semantics=("arbitrary", "arbitrary"),
                                             vmem_limit_bytes=VMEM_LIMIT_BYTES),
        name="ret_scan",
    )(qkv, qkv, qkv, gate, dec, xi, zeta, gc, gn_g.reshape(1, RET_HV), gn_b.reshape(1, RET_HV))


def _retention_block(h, pos, w_in, log1m_decay, gn_g, gn_b, w_out, ln_g, ln_b, batch, seq):
    qkv, gate = _ret_in(h, pos, w_in)
    gated = _ret_scan(qkv, gate, log1m_decay, gn_g, gn_b, batch, seq)
    return _proj_ln(gated, w_out, h, ln_g, ln_b)


MLA_HEADS = 8
MLA_NOPE = 128
MLA_ROPE = 64
MLA_VDIM = 128
MLA_Q_RANK = 384
MLA_KV_RANK = 256
MLA_QK_PAD = 256
MLA_Q_TILE = 256


def _rms(x, g):
    ms = jnp.mean(x * x, axis=-1, keepdims=True)
    return x * lax.rsqrt(ms + LN_EPS) * g


def _mla_in_kernel(x_ref, pos_ref, freq_ref, win_ref, qg_ref, kvg_ref, wuq_ref, wukv_ref, q_ref, k_ref, v_ref):
    c = jnp.dot(x_ref[...].astype(BF16), win_ref[...], preferred_element_type=F32)
    lane = lax.broadcasted_iota(I32, (1, LANES), 1)
    half = MLA_ROPE // 2
    ang = pos_ref[...].astype(F32) * freq_ref[...]
    cosv = jnp.where(lane < MLA_ROPE, jnp.cos(ang), 0.0)
    sinv = jnp.sin(ang)
    sinv = jnp.where(lane < half, -sinv, jnp.where(lane < MLA_ROPE, sinv, 0.0))

    def rope(y):
        swapped = pltpu.roll(y, half, 1) + pltpu.roll(y, LANES - half, 1)
        return y * cosv + swapped * sinv

    cq = _rms(c[:, :MLA_Q_RANK], qg_ref[...]).astype(BF16)
    ckv = _rms(c[:, MLA_Q_RANK:MLA_Q_RANK + MLA_KV_RANK], kvg_ref[...]).astype(BF16)
    k_rope = rope(c[:, MLA_Q_RANK + MLA_KV_RANK:]).astype(BF16)
    q = jnp.dot(cq, wuq_ref[...], preferred_element_type=F32)
    kv = jnp.dot(ckv, wukv_ref[...], preferred_element_type=F32)
    for hd in range(MLA_HEADS):
        o = hd * MLA_QK_PAD
        q_ref[:, o:o + MLA_NOPE] = q[:, o:o + MLA_NOPE].astype(BF16)
        q_ref[:, o + MLA_NOPE:o + MLA_QK_PAD] = rope(q[:, o + MLA_NOPE:o + MLA_QK_PAD]).astype(BF16)
        k_ref[:, o:o + MLA_NOPE] = kv[:, o:o + MLA_NOPE].astype(BF16)
        k_ref[:, o + MLA_NOPE:o + MLA_QK_PAD] = k_rope
        v_ref[:, hd * MLA_VDIM:(hd + 1) * MLA_VDIM] = kv[:, o + MLA_NOPE:o + MLA_QK_PAD].astype(BF16)


def _mla_in(x, pos, w_in, q_norm_g, kv_norm_g, w_uq, w_ukv):
    t, d = x.shape
    tm = ROW_TILE
    pad = MLA_QK_PAD - MLA_NOPE - MLA_ROPE
    win = jnp.pad(w_in, ((0, 0), (0, pad))).astype(BF16)
    wuq = jnp.pad(w_uq.reshape(MLA_Q_RANK, MLA_HEADS, MLA_NOPE + MLA_ROPE), ((0, 0), (0, 0), (0, pad)))
    wuq = wuq.reshape(MLA_Q_RANK, MLA_HEADS * MLA_QK_PAD).astype(BF16)
    wukv = w_ukv.astype(BF16)
    f = ROPE_BASE ** (-jnp.arange(0, MLA_ROPE, 2, dtype=F32) / MLA_ROPE)
    freq = jnp.concatenate([f, f, jnp.zeros((LANES - MLA_ROPE,), F32)]).reshape(1, LANES)
    full = lambda a: pl.BlockSpec(a.shape, lambda i: (0,) * a.ndim)
    qg = q_norm_g.reshape(1, -1)
    kvg = kv_norm_g.reshape(1, -1)
    wide = MLA_HEADS * MLA_QK_PAD
    return pl.pallas_call(
        _mla_in_kernel,
        grid=(t // tm,),
        in_specs=[pl.BlockSpec((tm, d), lambda i: (i, 0)), pl.BlockSpec((tm, 1), lambda i: (i, 0)),
                  full(freq), full(win), full(qg), full(kvg), full(wuq), full(wukv)],
        out_specs=[pl.BlockSpec((tm, wide), lambda i: (i, 0)), pl.BlockSpec((tm, wide), lambda i: (i, 0)),
                   pl.BlockSpec((tm, MLA_HEADS * MLA_VDIM), lambda i: (i, 0))],
        out_shape=[jax.ShapeDtypeStruct((t, wide), BF16), jax.ShapeDtypeStruct((t, wide), BF16),
                   jax.ShapeDtypeStruct((t, MLA_HEADS * MLA_VDIM), BF16)],
        compiler_params=pltpu.CompilerParams(dimension_semantics=("arbitrary",),
                                             vmem_limit_bytes=VMEM_LIMIT_BYTES),
        name="mla_in",
    )(x, pos.reshape(t, 1), freq, win, qg, kvg, wuq, wukv)


def _mla_attn_kernel(q_ref, k_ref, v_ref, o_ref):
    s = lax.dot_general(q_ref[...], k_ref[...], (((1,), (1,)), ((), ())), preferred_element_type=F32)
    s = s * ((MLA_NOPE + MLA_ROPE) ** -0.5)
    p = jnp.exp(s - jnp.max(s, axis=-1, keepdims=True))
    denom = jnp.sum(p, axis=-1, keepdims=True)
    o = jnp.dot(p.astype(BF16), v_ref[...], preferred_element_type=F32)
    o_ref[...] = (o / denom).astype(BF16)


def _mla_attn(q, k, v, batch, seq):
    t = batch * seq
    tq = MLA_Q_TILE
    nq = seq // tq
    return pl.pallas_call(
        _mla_attn_kernel,
        grid=(batch, MLA_HEADS, nq),
        in_specs=[pl.BlockSpec((tq, MLA_QK_PAD), lambda b, h, i: (b * nq + i, h)),
                  pl.BlockSpec((seq, MLA_QK_PAD), lambda b, h, i: (b, h)),
                  pl.BlockSpec((seq, MLA_VDIM), lambda b, h, i: (b, h))],
        out_specs=pl.BlockSpec((tq, MLA_VDIM), lambda b, h, i: (b * nq + i, h)),
        out_shape=jax.ShapeDtypeStruct((t, MLA_HEADS * MLA_VDIM), BF16),
        compiler_params=pltpu.CompilerParams(dimension_semantics=("arbitrary", "arbitrary", "arbitrary"),
                                             vmem_limit_bytes=VMEM_LIMIT_BYTES),
        name="mla_attn",
    )(q, k, v)


def _mla_block(h, pos, w_in, q_norm_g, kv_norm_g, w_uq, w_ukv, w_out, ln_g, ln_b, batch, seq):
    q, k, v = _mla_in(h, pos, w_in, q_norm_g, kv_norm_g, w_uq, w_ukv)
    o = _mla_attn(q, k, v, batch, seq)
    return _proj_ln(o, w_out, h, ln_g, ln_b)


def kernel(x, positions, ret_w_in, ret_log1m_decay, ret_gn_g, ret_gn_b, ret_w_out, mla_w_in, mla_q_norm,
           mla_kv_norm, mla_w_uq, mla_w_ukv, mla_w_out, peer_w_q, peer_sub_keys, peer_u, peer_v, ln_mix_g,
           ln_mix_b, ln_ffn_g, ln_ffn_b):
    b, s, d = x.shape
    t = b * s
    h = x.reshape(t, d)
    pos = positions.reshape(t)
    for i in range(DEPTH):
        j = i // 2
        if i % 2 == 0:
            h = _retention_block(h, pos, ret_w_in[j], ret_log1m_decay[j], ret_gn_g[j], ret_gn_b[j],
                                 ret_w_out[j], ln_mix_g[i], ln_mix_b[i], b, s)
        else:
            h = _mla_block(h, pos, mla_w_in[j], mla_q_norm[j], mla_kv_norm[j], mla_w_uq[j], mla_w_ukv[j],
                           mla_w_out[j], ln_mix_g[i], ln_mix_b[i], b, s)
        ffn = _peer_ffn(h, peer_w_q[i], peer_sub_keys[i], peer_u[i], peer_v[i])
        h = _add_ln(h, ffn, ln_ffn_g[i], ln_ffn_b[i])
    return h.reshape(b, s, d)
```

```python
import functools

import jax
import jax.numpy as jnp
from jax import lax
from jax.experimental import pallas as pl
from jax.experimental.pallas import tpu as pltpu

F32 = jnp.float32
BF16 = jnp.bfloat16
I32 = jnp.int32
U32 = jnp.uint32

D_MODEL = 1024
DEPTH = 2
DN_ALPHA = (2.0 * DEPTH) ** 0.25
LN_EPS = 1e-5

PEER_HEADS = 8
PEER_NKEYS = 128
PEER_HALF = 128
PEER_TOPK = 16
PEER_SLOTS = PEER_HEADS * PEER_TOPK
PEER_EXPERTS = PEER_NKEYS * PEER_NKEYS
PEER_HALF_EXPERTS = PEER_EXPERTS // 2

SUBLANES = 8
LANES = 128
VMEM_LIMIT_BYTES = 56 * 1024 * 1024

ROW_TILE = 512
SEL_TOKENS = 256
GATHER_TOKENS = LANES


def _top_rows(st, ids, k, payload=None):
    vals, sel_ids, sel_pay = [], [], []
    big = jnp.int32(2 ** 30)
    for _ in range(k):
        m = jnp.max(st, axis=0, keepdims=True)
        sel = jnp.min(jnp.where(st == m, ids, big), axis=0, keepdims=True)
        hit = ids == sel
        vals.append(m)
        sel_ids.append(sel)
        if payload is not None:
            sel_pay.append(jnp.max(jnp.where(hit, payload, -1), axis=0, keepdims=True))
        st = jnp.where(hit, -jnp.inf, st)
    return vals, sel_ids, sel_pay


def _peer_select_kernel(h_ref, wq_ref, keys_ref, row_ref, up_ref, gate_ref, s_buf, i_buf, e_buf, g_buf):
    tb = h_ref.shape[0]
    q = jnp.dot(h_ref[...].astype(BF16), wq_ref[...], preferred_element_type=F32)
    key_ids = lax.broadcasted_iota(I32, (PEER_NKEYS, tb), 0)
    for head in range(PEER_HEADS):
        for c in range(2):
            col = (head * 2 + c) * PEER_HALF
            qhc = q[:, col:col + PEER_HALF].astype(BF16)
            st = lax.dot_general(keys_ref[head, c], qhc, (((1,), (1,)), ((), ())),
                                 preferred_element_type=F32)
            vals, ids, _ = _top_rows(st, key_ids, PEER_TOPK)
            for a in range(PEER_TOPK):
                s_buf[c, pl.ds(a, 1), :] = vals[a]
                i_buf[c, pl.ds(a, 1), :] = ids[a]
        cand_s, cand_pos, cand_e = [], [], []
        for b in range(PEER_TOPK):
            na = PEER_TOPK if b == 0 else SUBLANES
            cand_s.append(s_buf[0, :na, :] + s_buf[1, pl.ds(b, 1), :])
            cand_pos.append(lax.broadcasted_iota(I32, (na, tb), 0) * PEER_TOPK + b)
            cand_e.append(i_buf[0, :na, :] * PEER_NKEYS + i_buf[1, pl.ds(b, 1), :])
        cand_s = jnp.concatenate(cand_s, axis=0)
        cand_pos = jnp.concatenate(cand_pos, axis=0)
        cand_e = jnp.concatenate(cand_e, axis=0)
        top_s, _, top_e = _top_rows(cand_s, cand_pos, PEER_TOPK, payload=cand_e)
        ex = [jnp.exp(t - top_s[0]) for t in top_s]
        denom = functools.reduce(lambda x, y: x + y, ex)
        inv = 1.0 / denom
        for k in range(PEER_TOPK):
            slot = head * PEER_TOPK + k
            e_buf[pl.ds(slot, 1), :] = top_e[k]
            g_buf[pl.ds(slot, 1), :] = ex[k] * inv
    for s in range(tb // LANES):
        cols = slice(s * LANES, (s + 1) * LANES)
        e = e_buf[:, cols]
        row_ref[s] = (e & (PEER_HALF_EXPERTS - 1)) * SUBLANES
        up_ref[cols, :] = jnp.where(e >= PEER_HALF_EXPERTS, 1.0, 0.0).T
        gate_ref[cols, :] = g_buf[:, cols].T


def _peer_select(h, w_q, sub_keys):
    t = h.shape[0]
    tb = SEL_TOKENS
    wq = w_q.astype(BF16)
    keys = sub_keys.astype(BF16)
    tok = pl.BlockSpec((tb, PEER_SLOTS), lambda i: (i, 0))
    return pl.pallas_call(
        _peer_select_kernel,
        grid=(t // tb,),
        in_specs=[pl.BlockSpec((tb, D_MODEL), lambda i: (i, 0)),
                  pl.BlockSpec(wq.shape, lambda i: (0, 0)),
                  pl.BlockSpec(keys.shape, lambda i: (0, 0, 0, 0))],
        out_specs=[pl.BlockSpec((tb // LANES, PEER_SLOTS, LANES), lambda i: (i, 0, 0)), tok, tok],
        out_shape=[jax.ShapeDtypeStruct((t // LANES, PEER_SLOTS, LANES), I32),
                   jax.ShapeDtypeStruct((t, PEER_SLOTS), F32),
                   jax.ShapeDtypeStruct((t, PEER_SLOTS), F32)],
        scratch_shapes=[pltpu.VMEM((2, PEER_TOPK, tb), F32), pltpu.VMEM((2, PEER_TOPK, tb), I32),
                        pltpu.VMEM((PEER_SLOTS, tb), I32), pltpu.VMEM((PEER_SLOTS, tb), F32)],
        compiler_params=pltpu.CompilerParams(dimension_semantics=("arbitrary",),
                                             vmem_limit_bytes=VMEM_LIMIT_BYTES),
        name="peer_select",
    )(h, wq, keys)


def _pack_table(w):
    bits = lax.bitcast_convert_type(w.astype(BF16), jnp.uint16).astype(U32)
    packed = bits[:PEER_HALF_EXPERTS] | (bits[PEER_HALF_EXPERTS:] << 16)
    return packed.reshape(PEER_HALF_EXPERTS * SUBLANES, LANES)


HIGH_HALF = 0xFFFF0000


def _packed_rowsum(ws, sub):
    def add(a, b):
        return pltpu.bitcast(pltpu.bitcast(a, BF16) + pltpu.bitcast(b, BF16), U32)

    m4 = sub < 4
    lvl1 = []
    for j in range(4):
        x, y = ws[j], ws[j + 4]
        lvl1.append(add(jnp.where(m4, x, y), pltpu.roll(jnp.where(m4, y, x), 4, 0)))
    m2 = (sub & 2) == 0
    lvl2 = []
    for j in range(2):
        x, y = lvl1[j], lvl1[j + 2]
        lvl2.append(add(jnp.where(m2, x, pltpu.roll(y, 2, 0)), jnp.where(m2, pltpu.roll(x, 6, 0), y)))
    m1 = (sub & 1) == 0
    x, y = lvl2[0], lvl2[1]
    return add(jnp.where(m1, x, pltpu.roll(y, 1, 0)), jnp.where(m1, pltpu.roll(x, 7, 0), y))


def _peer_u_kernel(row_ref, x_ref, up_ref, gate_ref, tbl_ref, c_ref):
    tt = gate_ref.shape[0]
    n_chain = 2
    sub = lax.broadcasted_iota(I32, (SUBLANES, LANES), 0)
    lane = lax.broadcasted_iota(I32, (SUBLANES, LANES), 1)
    high = jnp.uint32(HIGH_HALF)

    def group(g, carry):
        t0 = pl.multiple_of(g * SUBLANES, SUBLANES)
        xs = []
        for j in range(SUBLANES):
            xrows = pl.ds(pl.multiple_of((t0 + j) * SUBLANES, SUBLANES), SUBLANES)
            bits = pltpu.bitcast(x_ref[xrows, :].astype(BF16).astype(F32), U32)
            xs.append(pltpu.bitcast(bits | (bits >> 16), BF16))
        lo = [jnp.zeros((SUBLANES, LANES), F32) for _ in range(n_chain)]
        hi = [jnp.zeros((SUBLANES, LANES), F32) for _ in range(n_chain)]
        for p in range(PEER_SLOTS):
            ws = []
            for j in range(SUBLANES):
                row8 = pl.multiple_of(row_ref[t0 + (p * tt + j)], SUBLANES)
                w = pltpu.bitcast(tbl_ref[pl.ds(row8, SUBLANES), :], BF16)
                ws.append(pltpu.bitcast(w * xs[j], U32))
            r = _packed_rowsum(ws, sub)
            here = lane == p
            c = p % n_chain
            lo[c] = jnp.where(here, jnp.sum(pltpu.bitcast(r << 16, F32), axis=1, keepdims=True), lo[c])
            hi[c] = jnp.where(here, jnp.sum(pltpu.bitcast(r & high, F32), axis=1, keepdims=True), hi[c])
        rows = pl.ds(t0, SUBLANES)
        hdot = jnp.where(up_ref[rows, :] > 0.5, hi[0] + hi[1], lo[0] + lo[1])
        act = 0.5 * hdot * (1.0 + lax.erf(hdot * (2.0 ** -0.5)))
        c_ref[rows, :] = act * gate_ref[rows, :]
        return carry

    lax.fori_loop(0, tt // SUBLANES, group, 0)


def _peer_v_kernel(row_ref, up_ref, c_ref, tbl_ref, o_ref):
    tt = c_ref.shape[0]
    lane = lax.broadcasted_iota(I32, (SUBLANES, LANES), 1)
    high = jnp.uint32(HIGH_HALF)
    full = (SUBLANES, LANES)

    def group(g, carry):
        t0 = pl.multiple_of(g * SUBLANES, SUBLANES)
        rows = pl.ds(t0, SUBLANES)
        uppers = up_ref[rows, :]
        coefs = c_ref[rows, :]
        accs = [jnp.zeros(full, F32) for _ in range(SUBLANES)]
        for p in range(PEER_SLOTS):
            here = lane == p
            coef = jnp.broadcast_to(jnp.sum(jnp.where(here, coefs, 0.0), axis=1, keepdims=True), full)
            upper = jnp.broadcast_to(jnp.sum(jnp.where(here, uppers, 0.0), axis=1, keepdims=True), full)
            shift = jnp.where(upper > 0.5, jnp.uint32(0), jnp.uint32(16))
            for j in range(SUBLANES):
                row8 = pl.multiple_of(row_ref[t0 + (p * tt + j)], SUBLANES)
                w = tbl_ref[pl.ds(row8, SUBLANES), :]
                sj = jnp.broadcast_to(shift[j:j + 1, :], full)
                cj = jnp.broadcast_to(coef[j:j + 1, :], full)
                accs[j] = accs[j] + cj * pltpu.bitcast((w << sj) & high, F32)
        for j in range(SUBLANES):
            o_ref[pl.ds(pl.multiple_of((t0 + j) * SUBLANES, SUBLANES), SUBLANES), :] = accs[j]
        return carry

    lax.fori_loop(0, tt // SUBLANES, group, 0)


def _peer_retrieve(x8, row8, upper, gate, u_tbl, v_tbl):
    t = gate.shape[0]
    tt = GATHER_TOKENS
    smem_spec = pl.BlockSpec((tt * PEER_SLOTS,), lambda i: (i,), memory_space=pltpu.SMEM)
    slots = pl.BlockSpec((tt, PEER_SLOTS), lambda i: (i, 0))
    rows8 = pl.BlockSpec((tt * SUBLANES, LANES), lambda i: (i, 0))
    table_spec = pl.BlockSpec(memory_space=pltpu.VMEM)
    params = pltpu.CompilerParams(dimension_semantics=("arbitrary",), vmem_limit_bytes=VMEM_LIMIT_BYTES)
    row8 = row8.reshape(-1)
    coef = pl.pallas_call(
        _peer_u_kernel,
        grid=(t // tt,),
        in_specs=[smem_spec, rows8, slots, slots, table_spec],
        out_specs=slots,
        out_shape=jax.ShapeDtypeStruct((t, PEER_SLOTS), F32),
        compiler_params=params,
        name="peer_u",
    )(row8, x8, upper, gate, u_tbl)
    return pl.pallas_call(
        _peer_v_kernel,
        grid=(t // tt,),
        in_specs=[smem_spec, slots, slots, table_spec],
        out_specs=rows8,
        out_shape=jax.ShapeDtypeStruct((t * SUBLANES, LANES), F32),
        compiler_params=params,
        name="peer_v",
    )(row8, upper, coef, v_tbl)


def _layer_norm(y, g, b):
    mu = jnp.mean(y, axis=-1, keepdims=True)
    d = y - mu
    var = jnp.mean(d * d, axis=-1, keepdims=True)
    return d * lax.rsqrt(var + LN_EPS) * g + b


def _store_tiles(o8_ref, out):
    tm = out.shape[0]
    for c in range(D_MODEL // LANES):
        o8_ref[pl.ds(c, tm, stride=SUBLANES), :] = out[:, c * LANES:(c + 1) * LANES]


def _add_ln_kernel(h_ref, f8_ref, g_ref, b_ref, o_ref):
    tm = h_ref.shape[0]
    f = jnp.concatenate([f8_ref[pl.ds(c, tm, stride=SUBLANES), :] for c in range(D_MODEL // LANES)], axis=1)
    o_ref[...] = _layer_norm(DN_ALPHA * h_ref[...] + f, g_ref[...], b_ref[...])


def _add_ln(h, f8, g, b):
    t, d = h.shape
    tb = ROW_TILE
    row = pl.BlockSpec((tb, d), lambda i: (i, 0))
    vec = pl.BlockSpec((1, d), lambda i: (0, 0))
    return pl.pallas_call(
        _add_ln_kernel,
        grid=(t // tb,),
        in_specs=[row, pl.BlockSpec((tb * SUBLANES, LANES), lambda i: (i, 0)), vec, vec],
        out_specs=row,
        out_shape=jax.ShapeDtypeStruct((t, d), F32),
        compiler_params=pltpu.CompilerParams(dimension_semantics=("arbitrary",)),
        name="add_ln",
    )(h, f8, g.reshape(1, d), b.reshape(1, d))


def _peer_block(h, h8, w_q, sub_keys, u, v, ln_g, ln_b):
    row8, upper, gate = _peer_select(h, w_q, sub_keys)
    f8 = _peer_retrieve(h8, row8, upper, gate, _pack_table(u), _pack_table(v))
    return _add_ln(h, f8, ln_g, ln_b)


def _proj_ln_kernel(a_ref, w_ref, h_ref, g_ref, b_ref, o_ref, o8_ref):
    y = DN_ALPHA * h_ref[...] + jnp.dot(a_ref[...], w_ref[...], preferred_element_type=F32)
    out = _layer_norm(y, g_ref[...], b_ref[...])
    o_ref[...] = out
    _store_tiles(o8_ref, out)


def _proj_ln(a, w, h, g, b):
    t, k = a.shape
    d = w.shape[1]
    tm = ROW_TILE
    vec = pl.BlockSpec((1, d), lambda i: (0, 0))
    return pl.pallas_call(
        _proj_ln_kernel,
        grid=(t // tm,),
        in_specs=[pl.BlockSpec((tm, k), lambda i: (i, 0)), pl.BlockSpec((k, d), lambda i: (0, 0)),
                  pl.BlockSpec((tm, d), lambda i: (i, 0)), vec, vec],
        out_specs=[pl.BlockSpec((tm, d), lambda i: (i, 0)),
                   pl.BlockSpec((tm * SUBLANES, LANES), lambda i: (i, 0))],
        out_shape=[jax.ShapeDtypeStruct((t, d), F32), jax.ShapeDtypeStruct((t * SUBLANES, LANES), F32)],
        compiler_params=pltpu.CompilerParams(dimension_semantics=("arbitrary",),
                                             vmem_limit_bytes=VMEM_LIMIT_BYTES),
        name="proj_ln",
    )(a, w.astype(BF16), h, g.reshape(1, d), b.reshape(1, d))


RET_HEADS = 4
RET_DK = 256
RET_DV = 512
RET_HK = RET_HEADS * RET_DK
RET_HV = RET_HEADS * RET_DV
RET_CHUNK = 256
ROPE_BASE = 10000.0
RET_COL_TILE = 256


def _ret_in_kernel(x_ref, pos_ref, freq_ref, w_ref, qkv_ref, gate_ref):
    tn = RET_COL_TILE
    n_qk = 2 * RET_HK // tn
    n_qkv = n_qk + RET_HV // tn
    half = RET_DK // 2
    x = x_ref[...].astype(BF16)
    ang = pos_ref[...].astype(F32) * freq_ref[...]
    c, s = jnp.cos(ang), jnp.sin(ang)
    for j in range(w_ref.shape[1] // tn):
        y = jnp.dot(x, w_ref[:, j * tn:(j + 1) * tn], preferred_element_type=F32)
        if j < n_qk:
            x1, x2 = y[:, :half], y[:, half:]
            scale = 1.0 if j < n_qk // 2 else RET_DK ** -0.5
            qkv_ref[:, j * tn:j * tn + half] = ((x1 * c - x2 * s) * scale).astype(BF16)
            qkv_ref[:, j * tn + half:(j + 1) * tn] = ((x1 * s + x2 * c) * scale).astype(BF16)
        elif j < n_qkv:
            qkv_ref[:, j * tn:(j + 1) * tn] = y.astype(BF16)
        else:
            gate_ref[:, (j - n_qkv) * tn:(j - n_qkv + 1) * tn] = y


def _ret_in(x, pos, w_in):
    t, d = x.shape
    tm = ROW_TILE
    n_all = w_in.shape[1]
    n_qkv = 2 * RET_HK + RET_HV
    half = RET_DK // 2
    freq = (ROPE_BASE ** (-jnp.arange(0, RET_DK, 2, dtype=F32) / RET_DK)).reshape(1, half)
    return pl.pallas_call(
        _ret_in_kernel,
        grid=(t // tm,),
        in_specs=[pl.BlockSpec((tm, d), lambda i: (i, 0)),
                  pl.BlockSpec((tm, 1), lambda i: (i, 0)),
                  pl.BlockSpec((1, half), lambda i: (0, 0)),
                  pl.BlockSpec((d, n_all), lambda i: (0, 0))],
        out_specs=[pl.BlockSpec((tm, n_qkv), lambda i: (i, 0)),
                   pl.BlockSpec((tm, n_all - n_qkv), lambda i: (i, 0))],
        out_shape=[jax.ShapeDtypeStruct((t, n_qkv), BF16),
                   jax.ShapeDtypeStruct((t, n_all - n_qkv), F32)],
        compiler_params=pltpu.CompilerParams(dimension_semantics=("arbitrary",),
                                             vmem_limit_bytes=VMEM_LIMIT_BYTES),
        name="ret_in",
    )(x, pos.reshape(t, 1), freq, w_in.astype(BF16))


def _ret_scan_kernel(q_ref, k_ref, v_ref, gate_ref, dec_ref, xi_ref, zeta_ref, gc_ref, gng_ref, gnb_ref,
                     o_ref, state_ref, y_ref):
    c = RET_CHUNK
    n_chunks = q_ref.shape[0] // c

    def chunk_out(i, d):
        rows = pl.ds(pl.multiple_of(i * c, c), c)
        qi, ki, vi = q_ref[rows, :], k_ref[rows, :], v_ref[rows, :]
        sc = lax.dot_general(qi, ki, (((1,), (1,)), ((), ())), preferred_element_type=F32)
        inner = jnp.dot((sc * dec_ref[d]).astype(BF16), vi, preferred_element_type=F32)
        cross = jnp.dot(qi, state_ref[...].astype(BF16), preferred_element_type=F32) * xi_ref[d]
        kz = (ki.astype(F32) * zeta_ref[d]).astype(BF16)
        upd = lax.dot_general(kz, vi, (((0,), (0,)), ((), ())), preferred_element_type=F32)
        state_ref[...] = gc_ref[d] * state_ref[...] + upd
        return rows, inner + cross

    state_ref[...] = jnp.zeros_like(state_ref)

    def fwd(i, carry):
        rows, y = chunk_out(i, 0)
        y_ref[rows, :] = y
        return carry

    lax.fori_loop(0, n_chunks, fwd, 0)
    state_ref[...] = jnp.zeros_like(state_ref)

    def bwd(n, carry):
        rows, y = chunk_out(n_chunks - 1 - n, 1)
        y = y + y_ref[rows, :]
        mu = jnp.mean(y, axis=-1, keepdims=True)
        dlt = y - mu
        var = jnp.mean(dlt * dlt, axis=-1, keepdims=True)
        yn = dlt * lax.rsqrt(var + LN_EPS) * gng_ref[...] + gnb_ref[...]
        g = gate_ref[rows, :]
        o_ref[rows, :] = (g * (1.0 / (1.0 + jnp.exp(-g))) * yn).astype(BF16)
        return carry

    lax.fori_loop(0, n_chunks, bwd, 0)


def _ret_scan(qkv, gate, log1m_decay, gn_g, gn_b, batch, seq):
    c = RET_CHUNK
    log_gamma = jnp.log1p(-jnp.exp(log1m_decay.astype(F32)))
    idx = jnp.arange(c, dtype=F32)
    diff = idx[:, None] - idx[None, :]
    lg = log_gamma[:, :, None, None]
    dec_f = jnp.where(diff >= 0, jnp.exp(jnp.maximum(diff, 0.0) * lg[0]), 0.0)
    dec_b = jnp.where(diff < 0, jnp.exp(jnp.maximum(-diff, 0.0) * lg[1]), 0.0)
    dec = jnp.stack([dec_f, dec_b], axis=1)
    lgc = log_gamma[:, :, None]
    xi = jnp.stack([jnp.exp((idx + 1.0) * lgc[0]), jnp.exp((c - idx) * lgc[1])], axis=1)[..., None]
    zeta = jnp.stack([jnp.exp((c - 1.0 - idx) * lgc[0]), jnp.exp(idx * lgc[1])], axis=1)[..., None]
    gc = jnp.exp(c * log_gamma).T.reshape(RET_HEADS, 2, 1, 1)
    t = batch * seq
    kq = RET_HK // RET_DK
    kv = 2 * RET_HK // RET_DV
    head4 = lambda b, h: (h, 0, 0, 0)
    return pl.pallas_call(
        _ret_scan_kernel,
        grid=(batch, RET_HEADS),
        in_specs=[pl.BlockSpec((seq, RET_DK), lambda b, h: (b, h)),
                  pl.BlockSpec((seq, RET_DK), lambda b, h: (b, kq + h)),
                  pl.BlockSpec((seq, RET_DV), lambda b, h: (b, kv + h)),
                  pl.BlockSpec((seq, RET_DV), lambda b, h: (b, h)),
                  pl.BlockSpec((None, 2, c, c), head4),
                  pl.BlockSpec((None, 2, c, 1), head4),
                  pl.BlockSpec((None, 2, c, 1), head4),
                  pl.BlockSpec((None, 2, 1, 1), head4),
                  pl.BlockSpec((1, RET_DV), lambda b, h: (0, h)),
                  pl.BlockSpec((1, RET_DV), lambda b, h: (0, h))],
        out_specs=pl.BlockSpec((seq, RET_DV), lambda b, h: (b, h)),
        out_shape=jax.ShapeDtypeStruct((t, RET_HV), BF16),
        scratch_shapes=[pltpu.VMEM((RET_DK, RET_DV), F32), pltpu.VMEM((seq, RET_DV), F32)],
        compiler_params=pltpu.CompilerParams(dimension_semantics=("arbitrary", "arbitrary"),
                                             vmem_limit_bytes=VMEM_LIMIT_BYTES),
        name="ret_scan",
    )(qkv, qkv, qkv, gate, dec, xi, zeta, gc, gn_g.reshape(1, RET_HV), gn_b.reshape(1, RET_HV))


def _retention_block(h, pos, w_in, log1m_decay, gn_g, gn_b, w_out, ln_g, ln_b, batch, seq):
    qkv, gate = _ret_in(h, pos, w_in)
    gated = _ret_scan(qkv, gate, log1m_decay, gn_g, gn_b, batch, seq)
    return _proj_ln(gated, w_out, h, ln_g, ln_b)


MLA_HEADS = 8
MLA_NOPE = 128
MLA_ROPE = 64
MLA_VDIM = 128
MLA_Q_RANK = 384
MLA_KV_RANK = 256
MLA_QK_PAD = 256
MLA_Q_TILE = 256


def _rms(x, g):
    ms = jnp.mean(x * x, axis=-1, keepdims=True)
    return x * lax.rsqrt(ms + LN_EPS) * g


def _mla_in_kernel(x_ref, pos_ref, freq_ref, win_ref, qg_ref, kvg_ref, wuq_ref, wukv_ref, q_ref, k_ref, v_ref):
    c = jnp.dot(x_ref[...].astype(BF16), win_ref[...], preferred_element_type=F32)
    lane = lax.broadcasted_iota(I32, (1, LANES), 1)
    half = MLA_ROPE // 2
    ang = pos_ref[...].astype(F32) * freq_ref[...]
    cosv = jnp.where(lane < MLA_ROPE, jnp.cos(ang), 0.0)
    sinv = jnp.sin(ang)
    sinv = jnp.where(lane < half, -sinv, jnp.where(lane < MLA_ROPE, sinv, 0.0))

    def rope(y):
        swapped = pltpu.roll(y, half, 1) + pltpu.roll(y, LANES - half, 1)
        return y * cosv + swapped * sinv

    cq = _rms(c[:, :MLA_Q_RANK], qg_ref[...]).astype(BF16)
    ckv = _rms(c[:, MLA_Q_RANK:MLA_Q_RANK + MLA_KV_RANK], kvg_ref[...]).astype(BF16)
    k_rope = rope(c[:, MLA_Q_RANK + MLA_KV_RANK:]).astype(BF16)
    q = jnp.dot(cq, wuq_ref[...], preferred_element_type=F32)
    kv = jnp.dot(ckv, wukv_ref[...], preferred_element_type=F32)
    for hd in range(MLA_HEADS):
        o = hd * MLA_QK_PAD
        q_ref[:, o:o + MLA_NOPE] = q[:, o:o + MLA_NOPE].astype(BF16)
        q_ref[:, o + MLA_NOPE:o + MLA_QK_PAD] = rope(q[:, o + MLA_NOPE:o + MLA_QK_PAD]).astype(BF16)
        k_ref[:, o:o + MLA_NOPE] = kv[:, o:o + MLA_NOPE].astype(BF16)
        k_ref[:, o + MLA_NOPE:o + MLA_QK_PAD] = k_rope
        v_ref[:, hd * MLA_VDIM:(hd + 1) * MLA_VDIM] = kv[:, o + MLA_NOPE:o + MLA_QK_PAD].astype(BF16)


def _mla_in(x, pos, w_in, q_norm_g, kv_norm_g, w_uq, w_ukv):
    t, d = x.shape
    tm = ROW_TILE
    pad = MLA_QK_PAD - MLA_NOPE - MLA_ROPE
    win = jnp.pad(w_in, ((0, 0), (0, pad))).astype(BF16)
    wuq = jnp.pad(w_uq.reshape(MLA_Q_RANK, MLA_HEADS, MLA_NOPE + MLA_ROPE), ((0, 0), (0, 0), (0, pad)))
    wuq = wuq.reshape(MLA_Q_RANK, MLA_HEADS * MLA_QK_PAD).astype(BF16)
    wukv = w_ukv.astype(BF16)
    f = ROPE_BASE ** (-jnp.arange(0, MLA_ROPE, 2, dtype=F32) / MLA_ROPE)
    freq = jnp.concatenate([f, f, jnp.zeros((LANES - MLA_ROPE,), F32)]).reshape(1, LANES)
    full = lambda a: pl.BlockSpec(a.shape, lambda i: (0,) * a.ndim)
    qg = q_norm_g.reshape(1, -1)
    kvg = kv_norm_g.reshape(1, -1)
    wide = MLA_HEADS * MLA_QK_PAD
    return pl.pallas_call(
        _mla_in_kernel,
        grid=(t // tm,),
        in_specs=[pl.BlockSpec((tm, d), lambda i: (i, 0)), pl.BlockSpec((tm, 1), lambda i: (i, 0)),
                  full(freq), full(win), full(qg), full(kvg), full(wuq), full(wukv)],
        out_specs=[pl.BlockSpec((tm, wide), lambda i: (i, 0)), pl.BlockSpec((tm, wide), lambda i: (i, 0)),
                   pl.BlockSpec((tm, MLA_HEADS * MLA_VDIM), lambda i: (i, 0))],
        out_shape=[jax.ShapeDtypeStruct((t, wide), BF16), jax.ShapeDtypeStruct((t, wide), BF16),
                   jax.ShapeDtypeStruct((t, MLA_HEADS * MLA_VDIM), BF16)],
        compiler_params=pltpu.CompilerParams(dimension_semantics=("arbitrary",),
                                             vmem_limit_bytes=VMEM_LIMIT_BYTES),
        name="mla_in",
    )(x, pos.reshape(t, 1), freq, win, qg, kvg, wuq, wukv)


def _mla_attn_kernel(q_ref, k_ref, v_ref, o_ref):
    s = lax.dot_general(q_ref[...], k_ref[...], (((1,), (1,)), ((), ())), preferred_element_type=F32)
    s = s * ((MLA_NOPE + MLA_ROPE) ** -0.5)
    p = jnp.exp(s - jnp.max(s, axis=-1, keepdims=True))
    denom = jnp.sum(p, axis=-1, keepdims=True)
    o = jnp.dot(p.astype(BF16), v_ref[...], preferred_element_type=F32)
    o_ref[...] = (o / denom).astype(BF16)


def _mla_attn(q, k, v, batch, seq):
    t = batch * seq
    tq = MLA_Q_TILE
    nq = seq // tq
    return pl.pallas_call(
        _mla_attn_kernel,
        grid=(batch, MLA_HEADS, nq),
        in_specs=[pl.BlockSpec((tq, MLA_QK_PAD), lambda b, h, i: (b * nq + i, h)),
                  pl.BlockSpec((seq, MLA_QK_PAD), lambda b, h, i: (b, h)),
                  pl.BlockSpec((seq, MLA_VDIM), lambda b, h, i: (b, h))],
        out_specs=pl.BlockSpec((tq, MLA_VDIM), lambda b, h, i: (b * nq + i, h)),
        out_shape=jax.ShapeDtypeStruct((t, MLA_HEADS * MLA_VDIM), BF16),
        compiler_params=pltpu.CompilerParams(dimension_semantics=("arbitrary", "arbitrary", "arbitrary"),
                                             vmem_limit_bytes=VMEM_LIMIT_BYTES),
        name="mla_attn",
    )(q, k, v)


def _mla_block(h, pos, w_in, q_norm_g, kv_norm_g, w_uq, w_ukv, w_out, ln_g, ln_b, batch, seq):
    q, k, v = _mla_in(h, pos, w_in, q_norm_g, kv_norm_g, w_uq, w_ukv)
    o = _mla_attn(q, k, v, batch, seq)
    return _proj_ln(o, w_out, h, ln_g, ln_b)


def kernel(x, positions, ret_w_in, ret_log1m_decay, ret_gn_g, ret_gn_b, ret_w_out, mla_w_in, mla_q_norm,
           mla_kv_norm, mla_w_uq, mla_w_ukv, mla_w_out, peer_w_q, peer_sub_keys, peer_u, peer_v, ln_mix_g,
           ln_mix_b, ln_ffn_g, ln_ffn_b):
    b, s, d = x.shape
    t = b * s
    h = x.reshape(t, d)
    pos = positions.reshape(t)
    for i in range(DEPTH):
        j = i // 2
        if i % 2 == 0:
            h, h8 = _retention_block(h, pos, ret_w_in[j], ret_log1m_decay[j], ret_gn_g[j], ret_gn_b[j],
                                     ret_w_out[j], ln_mix_g[i], ln_mix_b[i], b, s)
        else:
            h, h8 = _mla_block(h, pos, mla_w_in[j], mla_q_norm[j], mla_kv_norm[j], mla_w_uq[j], mla_w_ukv[j],
                               mla_w_out[j], ln_mix_g[i], ln_mix_b[i], b, s)
        h = _peer_block(h, h8, peer_w_q[i], peer_sub_keys[i], peer_u[i], peer_v[i], ln_ffn_g[i], ln_ffn_b[i])
    return h.reshape(b, s, d)
```

```python
import functools

import jax
import jax.numpy as jnp
from jax import lax
from jax.experimental import pallas as pl
from jax.experimental.pallas import tpu as pltpu

F32 = jnp.float32
BF16 = jnp.bfloat16
I32 = jnp.int32
U32 = jnp.uint32

D_MODEL = 1024
DEPTH = 2
DN_ALPHA = (2.0 * DEPTH) ** 0.25
LN_EPS = 1e-5

PEER_HEADS = 8
PEER_NKEYS = 128
PEER_HALF = 128
PEER_TOPK = 16
PEER_SLOTS = PEER_HEADS * PEER_TOPK
PEER_EXPERTS = PEER_NKEYS * PEER_NKEYS
PEER_HALF_EXPERTS = PEER_EXPERTS // 2

SUBLANES = 8
LANES = 128
VMEM_LIMIT_BYTES = 56 * 1024 * 1024

ROW_TILE = 512
SEL_TOKENS = 256
GATHER_TOKENS = LANES


def _top_rows(st, ids, k, payload=None):
    vals, sel_ids, sel_pay = [], [], []
    big = jnp.int32(2 ** 30)
    for _ in range(k):
        m = jnp.max(st, axis=0, keepdims=True)
        sel = jnp.min(jnp.where(st == m, ids, big), axis=0, keepdims=True)
        hit = ids == sel
        vals.append(m)
        sel_ids.append(sel)
        if payload is not None:
            sel_pay.append(jnp.max(jnp.where(hit, payload, -1), axis=0, keepdims=True))
        st = jnp.where(hit, -jnp.inf, st)
    return vals, sel_ids, sel_pay


def _peer_select_kernel(h_ref, wq_ref, keys_ref, row_ref, up_ref, gate_ref, s_buf, i_buf, e_buf, g_buf):
    tb = h_ref.shape[0]
    q = jnp.dot(h_ref[...].astype(BF16), wq_ref[...], preferred_element_type=F32)
    key_ids = lax.broadcasted_iota(I32, (PEER_NKEYS, tb), 0)
    for head in range(PEER_HEADS):
        for c in range(2):
            col = (head * 2 + c) * PEER_HALF
            qhc = q[:, col:col + PEER_HALF].astype(BF16)
            st = lax.dot_general(keys_ref[head, c], qhc, (((1,), (1,)), ((), ())),
                                 preferred_element_type=F32)
            vals, ids, _ = _top_rows(st, key_ids, PEER_TOPK)
            for a in range(PEER_TOPK):
                s_buf[c, pl.ds(a, 1), :] = vals[a]
                i_buf[c, pl.ds(a, 1), :] = ids[a]
        cand_s, cand_pos, cand_e = [], [], []
        for b in range(PEER_TOPK):
            na = PEER_TOPK if b == 0 else SUBLANES
            cand_s.append(s_buf[0, :na, :] + s_buf[1, pl.ds(b, 1), :])
            cand_pos.append(lax.broadcasted_iota(I32, (na, tb), 0) * PEER_TOPK + b)
            cand_e.append(i_buf[0, :na, :] * PEER_NKEYS + i_buf[1, pl.ds(b, 1), :])
        cand_s = jnp.concatenate(cand_s, axis=0)
        cand_pos = jnp.concatenate(cand_pos, axis=0)
        cand_e = jnp.concatenate(cand_e, axis=0)
        top_s, _, top_e = _top_rows(cand_s, cand_pos, PEER_TOPK, payload=cand_e)
        ex = [jnp.exp(t - top_s[0]) for t in top_s]
        denom = functools.reduce(lambda x, y: x + y, ex)
        inv = 1.0 / denom
        for k in range(PEER_TOPK):
            slot = head * PEER_TOPK + k
            e_buf[pl.ds(slot, 1), :] = top_e[k]
            g_buf[pl.ds(slot, 1), :] = ex[k] * inv
    for s in range(tb // LANES):
        cols = slice(s * LANES, (s + 1) * LANES)
        e = e_buf[:, cols]
        row_ref[s] = (e & (PEER_HALF_EXPERTS - 1)) * SUBLANES
        up_ref[cols, :] = jnp.where(e >= PEER_HALF_EXPERTS, 1.0, 0.0).T
        gate_ref[cols, :] = g_buf[:, cols].T


def _peer_select(h, w_q, sub_keys):
    t = h.shape[0]
    tb = SEL_TOKENS
    wq = w_q.astype(BF16)
    keys = sub_keys.astype(BF16)
    tok = pl.BlockSpec((tb, PEER_SLOTS), lambda i: (i, 0))
    return pl.pallas_call(
        _peer_select_kernel,
        grid=(t // tb,),
        in_specs=[pl.BlockSpec((tb, D_MODEL), lambda i: (i, 0)),
                  pl.BlockSpec(wq.shape, lambda i: (0, 0)),
                  pl.BlockSpec(keys.shape, lambda i: (0, 0, 0, 0))],
        out_specs=[pl.BlockSpec((tb // LANES, PEER_SLOTS, LANES), lambda i: (i, 0, 0)), tok, tok],
        out_shape=[jax.ShapeDtypeStruct((t // LANES, PEER_SLOTS, LANES), I32),
                   jax.ShapeDtypeStruct((t, PEER_SLOTS), F32),
                   jax.ShapeDtypeStruct((t, PEER_SLOTS), F32)],
        scratch_shapes=[pltpu.VMEM((2, PEER_TOPK, tb), F32), pltpu.VMEM((2, PEER_TOPK, tb), I32),
                        pltpu.VMEM((PEER_SLOTS, tb), I32), pltpu.VMEM((PEER_SLOTS, tb), F32)],
        compiler_params=pltpu.CompilerParams(dimension_semantics=("arbitrary",),
                                             vmem_limit_bytes=VMEM_LIMIT_BYTES),
        name="peer_select",
    )(h, wq, keys)


def _pack_table(w):
    bits = lax.bitcast_convert_type(w.astype(BF16), jnp.uint16).astype(U32)
    packed = bits[:PEER_HALF_EXPERTS] | (bits[PEER_HALF_EXPERTS:] << 16)
    return packed.reshape(PEER_HALF_EXPERTS * SUBLANES, LANES)


HIGH_HALF = 0xFFFF0000


def _packed_rowsum(ws, sub):
    def add(a, b):
        return pltpu.bitcast(pltpu.bitcast(a, BF16) + pltpu.bitcast(b, BF16), U32)

    m4 = sub < 4
    lvl1 = []
    for j in range(4):
        x, y = ws[j], ws[j + 4]
        lvl1.append(add(jnp.where(m4, x, y), pltpu.roll(jnp.where(m4, y, x), 4, 0)))
    m2 = (sub & 2) == 0
    lvl2 = []
    for j in range(2):
        x, y = lvl1[j], lvl1[j + 2]
        lvl2.append(add(jnp.where(m2, x, pltpu.roll(y, 2, 0)), jnp.where(m2, pltpu.roll(x, 6, 0), y)))
    m1 = (sub & 1) == 0
    x, y = lvl2[0], lvl2[1]
    return add(jnp.where(m1, x, pltpu.roll(y, 1, 0)), jnp.where(m1, pltpu.roll(x, 7, 0), y))


def _peer_u_kernel(row_ref, x_ref, up_ref, gate_ref, tbl_ref, c_ref):
    tt = gate_ref.shape[0]
    n_chain = 2
    sub = lax.broadcasted_iota(I32, (SUBLANES, LANES), 0)
    lane = lax.broadcasted_iota(I32, (SUBLANES, LANES), 1)
    high = jnp.uint32(HIGH_HALF)

    def group(g):
        t0 = g * SUBLANES
        xs = []
        for j in range(SUBLANES):
            xrows = pl.ds((t0 + j) * SUBLANES, SUBLANES)
            bits = pltpu.bitcast(x_ref[xrows, :].astype(BF16).astype(F32), U32)
            xs.append(pltpu.bitcast(bits | (bits >> 16), BF16))
        lo = [jnp.zeros((SUBLANES, LANES), F32) for _ in range(n_chain)]
        hi = [jnp.zeros((SUBLANES, LANES), F32) for _ in range(n_chain)]
        for p in range(PEER_SLOTS):
            ws = []
            for j in range(SUBLANES):
                row8 = pl.multiple_of(row_ref[t0 + (p * tt + j)], SUBLANES)
                w = pltpu.bitcast(tbl_ref[pl.ds(row8, SUBLANES), :], BF16)
                ws.append(pltpu.bitcast(w * xs[j], U32))
            r = _packed_rowsum(ws, sub)
            here = lane == p
            c = p % n_chain
            lo[c] = jnp.where(here, jnp.sum(pltpu.bitcast(r << 16, F32), axis=1, keepdims=True), lo[c])
            hi[c] = jnp.where(here, jnp.sum(pltpu.bitcast(r & high, F32), axis=1, keepdims=True), hi[c])
        rows = pl.ds(t0, SUBLANES)
        hdot = jnp.where(up_ref[rows, :] > 0.5, hi[0] + hi[1], lo[0] + lo[1])
        act = 0.5 * hdot * (1.0 + lax.erf(hdot * (2.0 ** -0.5)))
        c_ref[rows, :] = act * gate_ref[rows, :]

    for g in range(tt // SUBLANES):
        group(g)


V_SLOT_RUN = 4


def _peer_v_kernel(row_ref, up_ref, c_ref, tbl_ref, o_ref):
    tt = c_ref.shape[0]
    lane = lax.broadcasted_iota(I32, (SUBLANES, LANES), 1)
    high = jnp.uint32(HIGH_HALF)
    full = (SUBLANES, LANES)

    def group(g):
        t0 = g * SUBLANES
        rows = pl.ds(t0, SUBLANES)
        uppers = up_ref[rows, :]
        coefs = c_ref[rows, :]
        acc_lo = [jnp.zeros(full, F32) for _ in range(SUBLANES)]
        acc_hi = [jnp.zeros(full, F32) for _ in range(SUBLANES)]
        for p0 in range(0, PEER_SLOTS, V_SLOT_RUN):
            prods = [[] for _ in range(SUBLANES)]
            for p in range(p0, p0 + V_SLOT_RUN):
                here = lane == p
                coef = jnp.broadcast_to(jnp.sum(jnp.where(here, coefs, 0.0), axis=1, keepdims=True), full)
                upper = jnp.broadcast_to(jnp.sum(jnp.where(here, uppers, 0.0), axis=1, keepdims=True), full)
                cbits = pltpu.bitcast(coef.astype(BF16).astype(F32), U32)
                cpair = jnp.where(upper > 0.5, cbits, cbits >> 16)
                for j in range(SUBLANES):
                    row8 = pl.multiple_of(row_ref[t0 + (p * tt + j)], SUBLANES)
                    w = pltpu.bitcast(tbl_ref[pl.ds(row8, SUBLANES), :], BF16)
                    cj = pltpu.bitcast(jnp.broadcast_to(cpair[j:j + 1, :], full), BF16)
                    prods[j].append(w * cj)
            for j in range(SUBLANES):
                pr = prods[j]
                while len(pr) > 1:
                    pr = [pr[i] + pr[i + 1] for i in range(0, len(pr), 2)]
                s = pltpu.bitcast(pr[0], U32)
                acc_lo[j] = acc_lo[j] + pltpu.bitcast(s << 16, F32)
                acc_hi[j] = acc_hi[j] + pltpu.bitcast(s & high, F32)
        for j in range(SUBLANES):
            o_ref[pl.ds((t0 + j) * SUBLANES, SUBLANES), :] = acc_lo[j] + acc_hi[j]

    for g in range(tt // SUBLANES):
        group(g)


def _peer_retrieve(x8, row8, upper, gate, u_tbl, v_tbl):
    t = gate.shape[0]
    tt = GATHER_TOKENS
    smem_spec = pl.BlockSpec((tt * PEER_SLOTS,), lambda i: (i,), memory_space=pltpu.SMEM,
                             pipeline_mode=pl.Buffered(1))
    slots = pl.BlockSpec((tt, PEER_SLOTS), lambda i: (i, 0))
    rows8 = pl.BlockSpec((tt * SUBLANES, LANES), lambda i: (i, 0))
    table_spec = pl.BlockSpec(memory_space=pltpu.VMEM)
    params = pltpu.CompilerParams(dimension_semantics=("arbitrary",), vmem_limit_bytes=VMEM_LIMIT_BYTES)
    row8 = row8.reshape(-1)
    coef = pl.pallas_call(
        _peer_u_kernel,
        grid=(t // tt,),
        in_specs=[smem_spec, rows8, slots, slots, table_spec],
        out_specs=slots,
        out_shape=jax.ShapeDtypeStruct((t, PEER_SLOTS), F32),
        compiler_params=params,
        name="peer_u",
    )(row8, x8, upper, gate, u_tbl)
    return pl.pallas_call(
        _peer_v_kernel,
        grid=(t // tt,),
        in_specs=[smem_spec, slots, slots, table_spec],
        out_specs=rows8,
        out_shape=jax.ShapeDtypeStruct((t * SUBLANES, LANES), F32),
        compiler_params=params,
        name="peer_v",
    )(row8, upper, coef, v_tbl)


def _layer_norm(y, g, b):
    mu = jnp.mean(y, axis=-1, keepdims=True)
    d = y - mu
    var = jnp.mean(d * d, axis=-1, keepdims=True)
    return d * lax.rsqrt(var + LN_EPS) * g + b


def _store_tiles(o8_ref, out):
    tm = out.shape[0]
    for c in range(D_MODEL // LANES):
        o8_ref[pl.ds(c, tm, stride=SUBLANES), :] = out[:, c * LANES:(c + 1) * LANES]


def _add_ln_kernel(h_ref, f8_ref, g_ref, b_ref, o_ref):
    tm = h_ref.shape[0]
    f = jnp.concatenate([f8_ref[pl.ds(c, tm, stride=SUBLANES), :] for c in range(D_MODEL // LANES)], axis=1)
    o_ref[...] = _layer_norm(DN_ALPHA * h_ref[...] + f, g_ref[...], b_ref[...])


def _add_ln(h, f8, g, b):
    t, d = h.shape
    tb = ROW_TILE
    row = pl.BlockSpec((tb, d), lambda i: (i, 0))
    vec = pl.BlockSpec((1, d), lambda i: (0, 0))
    return pl.pallas_call(
        _add_ln_kernel,
        grid=(t // tb,),
        in_specs=[row, pl.BlockSpec((tb * SUBLANES, LANES), lambda i: (i, 0)), vec, vec],
        out_specs=row,
        out_shape=jax.ShapeDtypeStruct((t, d), F32),
        compiler_params=pltpu.CompilerParams(dimension_semantics=("arbitrary",)),
        name="add_ln",
    )(h, f8, g.reshape(1, d), b.reshape(1, d))


def _peer_block(h, h8, w_q, sub_keys, u, v, ln_g, ln_b):
    row8, upper, gate = _peer_select(h, w_q, sub_keys)
    f8 = _peer_retrieve(h8, row8, upper, gate, _pack_table(u), _pack_table(v))
    return _add_ln(h, f8, ln_g, ln_b)


def _proj_ln_kernel(a_ref, w_ref, h_ref, g_ref, b_ref, o_ref, o8_ref):
    y = DN_ALPHA * h_ref[...] + jnp.dot(a_ref[...], w_ref[...], preferred_element_type=F32)
    out = _layer_norm(y, g_ref[...], b_ref[...])
    o_ref[...] = out
    _store_tiles(o8_ref, out)


def _proj_ln(a, w, h, g, b):
    t, k = a.shape
    d = w.shape[1]
    tm = ROW_TILE
    vec = pl.BlockSpec((1, d), lambda i: (0, 0))
    return pl.pallas_call(
        _proj_ln_kernel,
        grid=(t // tm,),
        in_specs=[pl.BlockSpec((tm, k), lambda i: (i, 0)), pl.BlockSpec((k, d), lambda i: (0, 0)),
                  pl.BlockSpec((tm, d), lambda i: (i, 0)), vec, vec],
        out_specs=[pl.BlockSpec((tm, d), lambda i: (i, 0)),
                   pl.BlockSpec((tm * SUBLANES, LANES), lambda i: (i, 0))],
        out_shape=[jax.ShapeDtypeStruct((t, d), F32), jax.ShapeDtypeStruct((t * SUBLANES, LANES), F32)],
        compiler_params=pltpu.CompilerParams(dimension_semantics=("arbitrary",),
                                             vmem_limit_bytes=VMEM_LIMIT_BYTES),
        name="proj_ln",
    )(a, w.astype(BF16), h, g.reshape(1, d), b.reshape(1, d))


RET_HEADS = 4
RET_DK = 256
RET_DV = 512
RET_HK = RET_HEADS * RET_DK
RET_HV = RET_HEADS * RET_DV
RET_CHUNK = 256
ROPE_BASE = 10000.0
RET_COL_TILE = 256


def _ret_in_kernel(x_ref, pos_ref, freq_ref, w_ref, qkv_ref, gate_ref):
    tn = RET_COL_TILE
    n_qk = 2 * RET_HK // tn
    n_qkv = n_qk + RET_HV // tn
    half = RET_DK // 2
    x = x_ref[...].astype(BF16)
    ang = pos_ref[...].astype(F32) * freq_ref[...]
    c, s = jnp.cos(ang), jnp.sin(ang)
    for j in range(w_ref.shape[1] // tn):
        y = jnp.dot(x, w_ref[:, j * tn:(j + 1) * tn], preferred_element_type=F32)
        if j < n_qk:
            x1, x2 = y[:, :half], y[:, half:]
            scale = 1.0 if j < n_qk // 2 else RET_DK ** -0.5
            qkv_ref[:, j * tn:j * tn + half] = ((x1 * c - x2 * s) * scale).astype(BF16)
            qkv_ref[:, j * tn + half:(j + 1) * tn] = ((x1 * s + x2 * c) * scale).astype(BF16)
        elif j < n_qkv:
            qkv_ref[:, j * tn:(j + 1) * tn] = y.astype(BF16)
        else:
            gate_ref[:, (j - n_qkv) * tn:(j - n_qkv + 1) * tn] = y


def _ret_in(x, pos, w_in):
    t, d = x.shape
    tm = ROW_TILE
    n_all = w_in.shape[1]
    n_qkv = 2 * RET_HK + RET_HV
    half = RET_DK // 2
    freq = (ROPE_BASE ** (-jnp.arange(0, RET_DK, 2, dtype=F32) / RET_DK)).reshape(1, half)
    return pl.pallas_call(
        _ret_in_kernel,
        grid=(t // tm,),
        in_specs=[pl.BlockSpec((tm, d), lambda i: (i, 0)),
                  pl.BlockSpec((tm, 1), lambda i: (i, 0)),
                  pl.BlockSpec((1, half), lambda i: (0, 0)),
                  pl.BlockSpec((d, n_all), lambda i: (0, 0))],
        out_specs=[pl.BlockSpec((tm, n_qkv), lambda i: (i, 0)),
                   pl.BlockSpec((tm, n_all - n_qkv), lambda i: (i, 0))],
        out_shape=[jax.ShapeDtypeStruct((t, n_qkv), BF16),
                   jax.ShapeDtypeStruct((t, n_all - n_qkv), F32)],
        compiler_params=pltpu.CompilerParams(dimension_semantics=("arbitrary",),
                                             vmem_limit_bytes=VMEM_LIMIT_BYTES),
        name="ret_in",
    )(x, pos.reshape(t, 1), freq, w_in.astype(BF16))


def _ret_scan_kernel(q_ref, k_ref, v_ref, gate_ref, dec_ref, xi_ref, zeta_ref, gc_ref, gng_ref, gnb_ref,
                     o_ref, state_ref, y_ref):
    c = RET_CHUNK
    n_chunks = q_ref.shape[0] // c

    def chunk_out(i, d):
        rows = pl.ds(pl.multiple_of(i * c, c), c)
        qi, ki, vi = q_ref[rows, :], k_ref[rows, :], v_ref[rows, :]
        sc = lax.dot_general(qi, ki, (((1,), (1,)), ((), ())), preferred_element_type=F32)
        inner = jnp.dot((sc * dec_ref[d]).astype(BF16), vi, preferred_element_type=F32)
        cross = jnp.dot(qi, state_ref[...].astype(BF16), preferred_element_type=F32) * xi_ref[d]
        kz = (ki.astype(F32) * zeta_ref[d]).astype(BF16)
        upd = lax.dot_general(kz, vi, (((0,), (0,)), ((), ())), preferred_element_type=F32)
        state_ref[...] = gc_ref[d] * state_ref[...] + upd
        return rows, inner + cross

    state_ref[...] = jnp.zeros_like(state_ref)

    def fwd(i, carry):
        rows, y = chunk_out(i, 0)
        y_ref[rows, :] = y
        return carry

    lax.fori_loop(0, n_chunks, fwd, 0)
    state_ref[...] = jnp.zeros_like(state_ref)

    def bwd(n, carry):
        rows, y = chunk_out(n_chunks - 1 - n, 1)
        y = y + y_ref[rows, :]
        mu = jnp.mean(y, axis=-1, keepdims=True)
        dlt = y - mu
        var = jnp.mean(dlt * dlt, axis=-1, keepdims=True)
        yn = dlt * lax.rsqrt(var + LN_EPS) * gng_ref[...] + gnb_ref[...]
        g = gate_ref[rows, :]
        o_ref[rows, :] = (g * (1.0 / (1.0 + jnp.exp(-g))) * yn).astype(BF16)
        return carry

    lax.fori_loop(0, n_chunks, bwd, 0)


def _ret_scan(qkv, gate, log1m_decay, gn_g, gn_b, batch, seq):
    c = RET_CHUNK
    log_gamma = jnp.log1p(-jnp.exp(log1m_decay.astype(F32)))
    idx = jnp.arange(c, dtype=F32)
    diff = idx[:, None] - idx[None, :]
    lg = log_gamma[:, :, None, None]
    dec_f = jnp.where(diff >= 0, jnp.exp(jnp.maximum(diff, 0.0) * lg[0]), 0.0)
    dec_b = jnp.where(diff < 0, jnp.exp(jnp.maximum(-diff, 0.0) * lg[1]), 0.0)
    dec = jnp.stack([dec_f, dec_b], axis=1)
    lgc = log_gamma[:, :, None]
    xi = jnp.stack([jnp.exp((idx + 1.0) * lgc[0]), jnp.exp((c - idx) * lgc[1])], axis=1)[..., None]
    zeta = jnp.stack([jnp.exp((c - 1.0 - idx) * lgc[0]), jnp.exp(idx * lgc[1])], axis=1)[..., None]
    gc = jnp.exp(c * log_gamma).T.reshape(RET_HEADS, 2, 1, 1)
    t = batch * seq
    kq = RET_HK // RET_DK
    kv = 2 * RET_HK // RET_DV
    head4 = lambda b, h: (h, 0, 0, 0)
    return pl.pallas_call(
        _ret_scan_kernel,
        grid=(batch, RET_HEADS),
        in_specs=[pl.BlockSpec((seq, RET_DK), lambda b, h: (b, h)),
                  pl.BlockSpec((seq, RET_DK), lambda b, h: (b, kq + h)),
                  pl.BlockSpec((seq, RET_DV), lambda b, h: (b, kv + h)),
                  pl.BlockSpec((seq, RET_DV), lambda b, h: (b, h)),
                  pl.BlockSpec((None, 2, c, c), head4),
                  pl.BlockSpec((None, 2, c, 1), head4),
                  pl.BlockSpec((None, 2, c, 1), head4),
                  pl.BlockSpec((None, 2, 1, 1), head4),
                  pl.BlockSpec((1, RET_DV), lambda b, h: (0, h)),
                  pl.BlockSpec((1, RET_DV), lambda b, h: (0, h))],
        out_specs=pl.BlockSpec((seq, RET_DV), lambda b, h: (b, h)),
        out_shape=jax.ShapeDtypeStruct((t, RET_HV), BF16),
        scratch_shapes=[pltpu.VMEM((RET_DK, RET_DV), F32), pltpu.VMEM((seq, RET_DV), F32)],
        compiler_params=pltpu.CompilerParams(dimension_semantics=("arbitrary", "arbitrary"),
                                             vmem_limit_bytes=VMEM_LIMIT_BYTES),
        name="ret_scan",
    )(qkv, qkv, qkv, gate, dec, xi, zeta, gc, gn_g.reshape(1, RET_HV), gn_b.reshape(1, RET_HV))


def _retention_block(h, pos, w_in, log1m_decay, gn_g, gn_b, w_out, ln_g, ln_b, batch, seq):
    qkv, gate = _ret_in(h, pos, w_in)
    gated = _ret_scan(qkv, gate, log1m_decay, gn_g, gn_b, batch, seq)
    return _proj_ln(gated, w_out, h, ln_g, ln_b)


MLA_HEADS = 8
MLA_NOPE = 128
MLA_ROPE = 64
MLA_VDIM = 128
MLA_Q_RANK = 384
MLA_KV_RANK = 256
MLA_QK_PAD = 256
MLA_Q_TILE = 256


def _rms(x, g):
    ms = jnp.mean(x * x, axis=-1, keepdims=True)
    return x * lax.rsqrt(ms + LN_EPS) * g


def _mla_in_kernel(x_ref, pos_ref, freq_ref, win_ref, qg_ref, kvg_ref, wuq_ref, wukv_ref, q_ref, k_ref, v_ref):
    c = jnp.dot(x_ref[...].astype(BF16), win_ref[...], preferred_element_type=F32)
    lane = lax.broadcasted_iota(I32, (1, LANES), 1)
    half = MLA_ROPE // 2
    ang = pos_ref[...].astype(F32) * freq_ref[...]
    cosv = jnp.where(lane < MLA_ROPE, jnp.cos(ang), 0.0)
    sinv = jnp.sin(ang)
    sinv = jnp.where(lane < half, -sinv, jnp.where(lane < MLA_ROPE, sinv, 0.0))

    def rope(y):
        swapped = pltpu.roll(y, half, 1) + pltpu.roll(y, LANES - half, 1)
        return y * cosv + swapped * sinv

    cq = _rms(c[:, :MLA_Q_RANK], qg_ref[...]).astype(BF16)
    ckv = _rms(c[:, MLA_Q_RANK:MLA_Q_RANK + MLA_KV_RANK], kvg_ref[...]).astype(BF16)
    k_rope = rope(c[:, MLA_Q_RANK + MLA_KV_RANK:]).astype(BF16)
    q = jnp.dot(cq, wuq_ref[...], preferred_element_type=F32)
    kv = jnp.dot(ckv, wukv_ref[...], preferred_element_type=F32)
    for hd in range(MLA_HEADS):
        o = hd * MLA_QK_PAD
        q_ref[:, o:o + MLA_NOPE] = q[:, o:o + MLA_NOPE].astype(BF16)
        q_ref[:, o + MLA_NOPE:o + MLA_QK_PAD] = rope(q[:, o + MLA_NOPE:o + MLA_QK_PAD]).astype(BF16)
        k_ref[:, o:o + MLA_NOPE] = kv[:, o:o + MLA_NOPE].astype(BF16)
        k_ref[:, o + MLA_NOPE:o + MLA_QK_PAD] = k_rope
        v_ref[:, hd * MLA_VDIM:(hd + 1) * MLA_VDIM] = kv[:, o + MLA_NOPE:o + MLA_QK_PAD].astype(BF16)


def _mla_in(x, pos, w_in, q_norm_g, kv_norm_g, w_uq, w_ukv):
    t, d = x.shape
    tm = ROW_TILE
    pad = MLA_QK_PAD - MLA_NOPE - MLA_ROPE
    win = jnp.pad(w_in, ((0, 0), (0, pad))).astype(BF16)
    wuq = jnp.pad(w_uq.reshape(MLA_Q_RANK, MLA_HEADS, MLA_NOPE + MLA_ROPE), ((0, 0), (0, 0), (0, pad)))
    wuq = wuq.reshape(MLA_Q_RANK, MLA_HEADS * MLA_QK_PAD).astype(BF16)
    wukv = w_ukv.astype(BF16)
    f = ROPE_BASE ** (-jnp.arange(0, MLA_ROPE, 2, dtype=F32) / MLA_ROPE)
    freq = jnp.concatenate([f, f, jnp.zeros((LANES - MLA_ROPE,), F32)]).reshape(1, LANES)
    full = lambda a: pl.BlockSpec(a.shape, lambda i: (0,) * a.ndim)
    qg = q_norm_g.reshape(1, -1)
    kvg = kv_norm_g.reshape(1, -1)
    wide = MLA_HEADS * MLA_QK_PAD
    return pl.pallas_call(
        _mla_in_kernel,
        grid=(t // tm,),
        in_specs=[pl.BlockSpec((tm, d), lambda i: (i, 0)), pl.BlockSpec((tm, 1), lambda i: (i, 0)),
                  full(freq), full(win), full(qg), full(kvg), full(wuq), full(wukv)],
        out_specs=[pl.BlockSpec((tm, wide), lambda i: (i, 0)), pl.BlockSpec((tm, wide), lambda i: (i, 0)),
                   pl.BlockSpec((tm, MLA_HEADS * MLA_VDIM), lambda i: (i, 0))],
        out_shape=[jax.ShapeDtypeStruct((t, wide), BF16), jax.ShapeDtypeStruct((t, wide), BF16),
                   jax.ShapeDtypeStruct((t, MLA_HEADS * MLA_VDIM), BF16)],
        compiler_params=pltpu.CompilerParams(dimension_semantics=("arbitrary",),
                                             vmem_limit_bytes=VMEM_LIMIT_BYTES),
        name="mla_in",
    )(x, pos.reshape(t, 1), freq, win, qg, kvg, wuq, wukv)


def _mla_attn_kernel(q_ref, k_ref, v_ref, o_ref):
    s = lax.dot_general(q_ref[...], k_ref[...], (((1,), (1,)), ((), ())), preferred_element_type=F32)
    s = s * ((MLA_NOPE + MLA_ROPE) ** -0.5)
    p = jnp.exp(s - jnp.max(s, axis=-1, keepdims=True))
    denom = jnp.sum(p, axis=-1, keepdims=True)
    o = jnp.dot(p.astype(BF16), v_ref[...], preferred_element_type=F32)
    o_ref[...] = (o / denom).astype(BF16)


def _mla_attn(q, k, v, batch, seq):
    t = batch * seq
    tq = MLA_Q_TILE
    nq = seq // tq
    return pl.pallas_call(
        _mla_attn_kernel,
        grid=(batch, MLA_HEADS, nq),
        in_specs=[pl.BlockSpec((tq, MLA_QK_PAD), lambda b, h, i: (b * nq + i, h)),
                  pl.BlockSpec((seq, MLA_QK_PAD), lambda b, h, i: (b, h)),
                  pl.BlockSpec((seq, MLA_VDIM), lambda b, h, i: (b, h))],
        out_specs=pl.BlockSpec((tq, MLA_VDIM), lambda b, h, i: (b * nq + i, h)),
        out_shape=jax.ShapeDtypeStruct((t, MLA_HEADS * MLA_VDIM), BF16),
        compiler_params=pltpu.CompilerParams(dimension_semantics=("arbitrary", "arbitrary", "arbitrary"),
                                             vmem_limit_bytes=VMEM_LIMIT_BYTES),
        name="mla_attn",
    )(q, k, v)


def _mla_block(h, pos, w_in, q_norm_g, kv_norm_g, w_uq, w_ukv, w_out, ln_g, ln_b, batch, seq):
    q, k, v = _mla_in(h, pos, w_in, q_norm_g, kv_norm_g, w_uq, w_ukv)
    o = _mla_attn(q, k, v, batch, seq)
    return _proj_ln(o, w_out, h, ln_g, ln_b)


def kernel(x, positions, ret_w_in, ret_log1m_decay, ret_gn_g, ret_gn_b, ret_w_out, mla_w_in, mla_q_norm,
           mla_kv_norm, mla_w_uq, mla_w_ukv, mla_w_out, peer_w_q, peer_sub_keys, peer_u, peer_v, ln_mix_g,
           ln_mix_b, ln_ffn_g, ln_ffn_b):
    b, s, d = x.shape
    t = b * s
    h = x.reshape(t, d)
    pos = positions.reshape(t)
    for i in range(DEPTH):
        j = i // 2
        if i % 2 == 0:
            h, h8 = _retention_block(h, pos, ret_w_in[j], ret_log1m_decay[j], ret_gn_g[j], ret_gn_b[j],
                                     ret_w_out[j], ln_mix_g[i], ln_mix_b[i], b, s)
        else:
            h, h8 = _mla_block(h, pos, mla_w_in[j], mla_q_norm[j], mla_kv_norm[j], mla_w_uq[j], mla_w_ukv[j],
                               mla_w_out[j], ln_mix_g[i], ln_mix_b[i], b, s)
        h = _peer_block(h, h8, peer_w_q[i], peer_sub_keys[i], peer_u[i], peer_v[i], ln_ffn_g[i], ln_ffn_b[i])
    return h.reshape(b, s, d)
```

```python
import functools

import jax
import jax.numpy as jnp
from jax import lax
from jax.experimental import pallas as pl
from jax.experimental.pallas import tpu as pltpu

F32 = jnp.float32
BF16 = jnp.bfloat16
I32 = jnp.int32
U32 = jnp.uint32

D_MODEL = 1024
DEPTH = 2
DN_ALPHA = (2.0 * DEPTH) ** 0.25
LN_EPS = 1e-5

PEER_HEADS = 8
PEER_NKEYS = 128
PEER_HALF = 128
PEER_TOPK = 16
PEER_SLOTS = PEER_HEADS * PEER_TOPK
PEER_EXPERTS = PEER_NKEYS * PEER_NKEYS
PEER_HALF_EXPERTS = PEER_EXPERTS // 2

SUBLANES = 8
LANES = 128
VMEM_LIMIT_BYTES = 56 * 1024 * 1024

ROW_TILE = 512
SEL_TOKENS = 256
GATHER_TOKENS = LANES


def _top_rows(st, ids, k, payload=None):
    vals, sel_ids, sel_pay = [], [], []
    big = jnp.int32(2 ** 30)
    for _ in range(k):
        m = jnp.max(st, axis=0, keepdims=True)
        sel = jnp.min(jnp.where(st == m, ids, big), axis=0, keepdims=True)
        hit = ids == sel
        vals.append(m)
        sel_ids.append(sel)
        if payload is not None:
            sel_pay.append(jnp.max(jnp.where(hit, payload, -1), axis=0, keepdims=True))
        st = jnp.where(hit, -jnp.inf, st)
    return vals, sel_ids, sel_pay


def _peer_select_kernel(h_ref, wq_ref, keys_ref, row_ref, up_ref, gate_ref, s_buf, i_buf, e_buf, g_buf):
    tb = h_ref.shape[0]
    q = jnp.dot(h_ref[...].astype(BF16), wq_ref[...], preferred_element_type=F32)
    key_ids = lax.broadcasted_iota(I32, (PEER_NKEYS, tb), 0)
    for head in range(PEER_HEADS):
        for c in range(2):
            col = (head * 2 + c) * PEER_HALF
            qhc = q[:, col:col + PEER_HALF].astype(BF16)
            st = lax.dot_general(keys_ref[head, c], qhc, (((1,), (1,)), ((), ())),
                                 preferred_element_type=F32)
            vals, ids, _ = _top_rows(st, key_ids, PEER_TOPK)
            for a in range(PEER_TOPK):
                s_buf[c, pl.ds(a, 1), :] = vals[a]
                i_buf[c, pl.ds(a, 1), :] = ids[a]
        cand_s, cand_pos, cand_e = [], [], []
        for b in range(PEER_TOPK):
            na = PEER_TOPK if b == 0 else SUBLANES
            cand_s.append(s_buf[0, :na, :] + s_buf[1, pl.ds(b, 1), :])
            cand_pos.append(lax.broadcasted_iota(I32, (na, tb), 0) * PEER_TOPK + b)
            cand_e.append(i_buf[0, :na, :] * PEER_NKEYS + i_buf[1, pl.ds(b, 1), :])
        cand_s = jnp.concatenate(cand_s, axis=0)
        cand_pos = jnp.concatenate(cand_pos, axis=0)
        cand_e = jnp.concatenate(cand_e, axis=0)
        top_s, _, top_e = _top_rows(cand_s, cand_pos, PEER_TOPK, payload=cand_e)
        ex = [jnp.exp(t - top_s[0]) for t in top_s]
        denom = functools.reduce(lambda x, y: x + y, ex)
        inv = 1.0 / denom
        for k in range(PEER_TOPK):
            slot = head * PEER_TOPK + k
            e_buf[pl.ds(slot, 1), :] = top_e[k]
            g_buf[pl.ds(slot, 1), :] = ex[k] * inv
    for s in range(tb // LANES):
        cols = slice(s * LANES, (s + 1) * LANES)
        e = e_buf[:, cols]
        row_ref[s] = (e & (PEER_HALF_EXPERTS - 1)) * SUBLANES
        up_ref[cols, :] = jnp.where(e >= PEER_HALF_EXPERTS, 1.0, 0.0).T
        gate_ref[cols, :] = g_buf[:, cols].T


def _peer_select(h, w_q, sub_keys):
    t = h.shape[0]
    tb = SEL_TOKENS
    wq = w_q.astype(BF16)
    keys = sub_keys.astype(BF16)
    tok = pl.BlockSpec((tb, PEER_SLOTS), lambda i: (i, 0))
    return pl.pallas_call(
        _peer_select_kernel,
        grid=(t // tb,),
        in_specs=[pl.BlockSpec((tb, D_MODEL), lambda i: (i, 0)),
                  pl.BlockSpec(wq.shape, lambda i: (0, 0)),
                  pl.BlockSpec(keys.shape, lambda i: (0, 0, 0, 0))],
        out_specs=[pl.BlockSpec((tb // LANES, PEER_SLOTS, LANES), lambda i: (i, 0, 0)), tok, tok],
        out_shape=[jax.ShapeDtypeStruct((t // LANES, PEER_SLOTS, LANES), I32),
                   jax.ShapeDtypeStruct((t, PEER_SLOTS), F32),
                   jax.ShapeDtypeStruct((t, PEER_SLOTS), F32)],
        scratch_shapes=[pltpu.VMEM((2, PEER_TOPK, tb), F32), pltpu.VMEM((2, PEER_TOPK, tb), I32),
                        pltpu.VMEM((PEER_SLOTS, tb), I32), pltpu.VMEM((PEER_SLOTS, tb), F32)],
        compiler_params=pltpu.CompilerParams(dimension_semantics=("arbitrary",),
                                             vmem_limit_bytes=VMEM_LIMIT_BYTES),
        name="peer_select",
    )(h, wq, keys)


HIGH_HALF = 0xFFFF0000
PACK_ROWS = 512


def _pack_table_kernel(lo_ref, hi_ref, o_ref):
    lo = pltpu.bitcast(lo_ref[...].astype(BF16).astype(F32), U32) >> 16
    hi = pltpu.bitcast(hi_ref[...].astype(BF16).astype(F32), U32) & jnp.uint32(HIGH_HALF)
    packed = lo | hi
    rows = packed.shape[0]
    for c in range(D_MODEL // LANES):
        o_ref[pl.ds(c, rows, stride=SUBLANES), :] = packed[:, c * LANES:(c + 1) * LANES]


def _pack_table(w):
    r = PACK_ROWS
    steps = PEER_HALF_EXPERTS // r
    return pl.pallas_call(
        _pack_table_kernel,
        grid=(steps,),
        in_specs=[pl.BlockSpec((r, D_MODEL), lambda i: (i, 0)),
                  pl.BlockSpec((r, D_MODEL), lambda i: (i + steps, 0))],
        out_specs=pl.BlockSpec((r * SUBLANES, LANES), lambda i: (i, 0)),
        out_shape=jax.ShapeDtypeStruct((PEER_HALF_EXPERTS * SUBLANES, LANES), U32),
        compiler_params=pltpu.CompilerParams(dimension_semantics=("arbitrary",),
                                             vmem_limit_bytes=VMEM_LIMIT_BYTES),
        name="pack_table",
    )(w, w)


def _packed_rowsum(ws, sub):
    def add(a, b):
        return pltpu.bitcast(pltpu.bitcast(a, BF16) + pltpu.bitcast(b, BF16), U32)

    m4 = sub < 4
    lvl1 = []
    for j in range(4):
        x, y = ws[j], ws[j + 4]
        lvl1.append(add(jnp.where(m4, x, y), pltpu.roll(jnp.where(m4, y, x), 4, 0)))
    m2 = (sub & 2) == 0
    lvl2 = []
    for j in range(2):
        x, y = lvl1[j], lvl1[j + 2]
        lvl2.append(add(jnp.where(m2, x, pltpu.roll(y, 2, 0)), jnp.where(m2, pltpu.roll(x, 6, 0), y)))
    m1 = (sub & 1) == 0
    x, y = lvl2[0], lvl2[1]
    return add(jnp.where(m1, x, pltpu.roll(y, 1, 0)), jnp.where(m1, pltpu.roll(x, 7, 0), y))


def _peer_u_kernel(row_ref, x_ref, up_ref, gate_ref, tbl_ref, c_ref):
    tt = gate_ref.shape[0]
    n_chain = 2
    sub = lax.broadcasted_iota(I32, (SUBLANES, LANES), 0)
    lane = lax.broadcasted_iota(I32, (SUBLANES, LANES), 1)
    high = jnp.uint32(HIGH_HALF)

    def group(g):
        t0 = g * SUBLANES
        xs = []
        for j in range(SUBLANES):
            xrows = pl.ds((t0 + j) * SUBLANES, SUBLANES)
            bits = pltpu.bitcast(x_ref[xrows, :].astype(BF16).astype(F32), U32)
            xs.append(pltpu.bitcast(bits | (bits >> 16), BF16))
        lo = [jnp.zeros((SUBLANES, LANES), F32) for _ in range(n_chain)]
        hi = [jnp.zeros((SUBLANES, LANES), F32) for _ in range(n_chain)]
        for p in range(PEER_SLOTS):
            ws = []
            for j in range(SUBLANES):
                row8 = pl.multiple_of(row_ref[t0 + (p * tt + j)], SUBLANES)
                w = pltpu.bitcast(tbl_ref[pl.ds(row8, SUBLANES), :], BF16)
                ws.append(pltpu.bitcast(w * xs[j], U32))
            r = _packed_rowsum(ws, sub)
            here = lane == p
            c = p % n_chain
            lo[c] = jnp.where(here, jnp.sum(pltpu.bitcast(r << 16, F32), axis=1, keepdims=True), lo[c])
            hi[c] = jnp.where(here, jnp.sum(pltpu.bitcast(r & high, F32), axis=1, keepdims=True), hi[c])
        rows = pl.ds(t0, SUBLANES)
        hdot = jnp.where(up_ref[rows, :] > 0.5, hi[0] + hi[1], lo[0] + lo[1])
        act = 0.5 * hdot * (1.0 + lax.erf(hdot * (2.0 ** -0.5)))
        c_ref[rows, :] = act * gate_ref[rows, :]

    for g in range(tt // SUBLANES):
        group(g)


V_SLOT_RUN = 4


def _peer_v_kernel(row_ref, up_ref, c_ref, tbl_ref, o_ref):
    tt = c_ref.shape[0]
    lane = lax.broadcasted_iota(I32, (SUBLANES, LANES), 1)
    high = jnp.uint32(HIGH_HALF)
    full = (SUBLANES, LANES)

    def group(g):
        t0 = g * SUBLANES
        rows = pl.ds(t0, SUBLANES)
        uppers = up_ref[rows, :]
        coefs = c_ref[rows, :]
        acc_lo = [jnp.zeros(full, F32) for _ in range(SUBLANES)]
        acc_hi = [jnp.zeros(full, F32) for _ in range(SUBLANES)]
        for p0 in range(0, PEER_SLOTS, V_SLOT_RUN):
            prods = [[] for _ in range(SUBLANES)]
            for p in range(p0, p0 + V_SLOT_RUN):
                here = lane == p
                coef = jnp.broadcast_to(jnp.sum(jnp.where(here, coefs, 0.0), axis=1, keepdims=True), full)
                upper = jnp.broadcast_to(jnp.sum(jnp.where(here, uppers, 0.0), axis=1, keepdims=True), full)
                cbits = pltpu.bitcast(coef.astype(BF16).astype(F32), U32)
                cpair = jnp.where(upper > 0.5, cbits, cbits >> 16)
                for j in range(SUBLANES):
                    row8 = pl.multiple_of(row_ref[t0 + (p * tt + j)], SUBLANES)
                    w = pltpu.bitcast(tbl_ref[pl.ds(row8, SUBLANES), :], BF16)
                    cj = pltpu.bitcast(jnp.broadcast_to(cpair[j:j + 1, :], full), BF16)
                    prods[j].append(w * cj)
            for j in range(SUBLANES):
                pr = prods[j]
                while len(pr) > 1:
                    pr = [pr[i] + pr[i + 1] for i in range(0, len(pr), 2)]
                s = pltpu.bitcast(pr[0], U32)
                acc_lo[j] = acc_lo[j] + pltpu.bitcast(s << 16, F32)
                acc_hi[j] = acc_hi[j] + pltpu.bitcast(s & high, F32)
        for j in range(SUBLANES):
            o_ref[pl.ds((t0 + j) * SUBLANES, SUBLANES), :] = acc_lo[j] + acc_hi[j]

    for g in range(tt // SUBLANES):
        group(g)


def _peer_retrieve(x8, row8, upper, gate, u_tbl, v_tbl):
    t = gate.shape[0]
    tt = GATHER_TOKENS
    smem_spec = pl.BlockSpec((tt * PEER_SLOTS,), lambda i: (i,), memory_space=pltpu.SMEM,
                             pipeline_mode=pl.Buffered(1))
    slots = pl.BlockSpec((tt, PEER_SLOTS), lambda i: (i, 0))
    rows8 = pl.BlockSpec((tt * SUBLANES, LANES), lambda i: (i, 0))
    table_spec = pl.BlockSpec(memory_space=pltpu.VMEM)
    params = pltpu.CompilerParams(dimension_semantics=("arbitrary",), vmem_limit_bytes=VMEM_LIMIT_BYTES)
    row8 = row8.reshape(-1)
    coef = pl.pallas_call(
        _peer_u_kernel,
        grid=(t // tt,),
        in_specs=[smem_spec, rows8, slots, slots, table_spec],
        out_specs=slots,
        out_shape=jax.ShapeDtypeStruct((t, PEER_SLOTS), F32),
        compiler_params=params,
        name="peer_u",
    )(row8, x8, upper, gate, u_tbl)
    return pl.pallas_call(
        _peer_v_kernel,
        grid=(t // tt,),
        in_specs=[smem_spec, slots, slots, table_spec],
        out_specs=rows8,
        out_shape=jax.ShapeDtypeStruct((t * SUBLANES, LANES), F32),
        compiler_params=params,
        name="peer_v",
    )(row8, upper, coef, v_tbl)


def _layer_norm(y, g, b):
    mu = jnp.mean(y, axis=-1, keepdims=True)
    d = y - mu
    var = jnp.mean(d * d, axis=-1, keepdims=True)
    return d * lax.rsqrt(var + LN_EPS) * g + b


def _store_tiles(o8_ref, out):
    tm = out.shape[0]
    for c in range(D_MODEL // LANES):
        o8_ref[pl.ds(c, tm, stride=SUBLANES), :] = out[:, c * LANES:(c + 1) * LANES]


def _add_ln_kernel(h_ref, f8_ref, g_ref, b_ref, o_ref):
    tm = h_ref.shape[0]
    f = jnp.concatenate([f8_ref[pl.ds(c, tm, stride=SUBLANES), :] for c in range(D_MODEL // LANES)], axis=1)
    o_ref[...] = _layer_norm(DN_ALPHA * h_ref[...] + f, g_ref[...], b_ref[...])


def _add_ln(h, f8, g, b):
    t, d = h.shape
    tb = ROW_TILE
    row = pl.BlockSpec((tb, d), lambda i: (i, 0))
    vec = pl.BlockSpec((1, d), lambda i: (0, 0))
    return pl.pallas_call(
        _add_ln_kernel,
        grid=(t // tb,),
        in_specs=[row, pl.BlockSpec((tb * SUBLANES, LANES), lambda i: (i, 0)), vec, vec],
        out_specs=row,
        out_shape=jax.ShapeDtypeStruct((t, d), F32),
        compiler_params=pltpu.CompilerParams(dimension_semantics=("arbitrary",)),
        name="add_ln",
    )(h, f8, g.reshape(1, d), b.reshape(1, d))


def _peer_block(h, h8, w_q, sub_keys, u, v, ln_g, ln_b):
    row8, upper, gate = _peer_select(h, w_q, sub_keys)
    f8 = _peer_retrieve(h8, row8, upper, gate, _pack_table(u), _pack_table(v))
    return _add_ln(h, f8, ln_g, ln_b)


def _proj_ln_kernel(a_ref, w_ref, h_ref, g_ref, b_ref, o_ref, o8_ref):
    y = DN_ALPHA * h_ref[...] + jnp.dot(a_ref[...], w_ref[...], preferred_element_type=F32)
    out = _layer_norm(y, g_ref[...], b_ref[...])
    o_ref[...] = out
    _store_tiles(o8_ref, out)


def _proj_ln(a, w, h, g, b):
    t, k = a.shape
    d = w.shape[1]
    tm = ROW_TILE
    vec = pl.BlockSpec((1, d), lambda i: (0, 0))
    return pl.pallas_call(
        _proj_ln_kernel,
        grid=(t // tm,),
        in_specs=[pl.BlockSpec((tm, k), lambda i: (i, 0)), pl.BlockSpec((k, d), lambda i: (0, 0)),
                  pl.BlockSpec((tm, d), lambda i: (i, 0)), vec, vec],
        out_specs=[pl.BlockSpec((tm, d), lambda i: (i, 0)),
                   pl.BlockSpec((tm * SUBLANES, LANES), lambda i: (i, 0))],
        out_shape=[jax.ShapeDtypeStruct((t, d), F32), jax.ShapeDtypeStruct((t * SUBLANES, LANES), F32)],
        compiler_params=pltpu.CompilerParams(dimension_semantics=("arbitrary",),
                                             vmem_limit_bytes=VMEM_LIMIT_BYTES),
        name="proj_ln",
    )(a, w.astype(BF16), h, g.reshape(1, d), b.reshape(1, d))


RET_HEADS = 4
RET_DK = 256
RET_DV = 512
RET_HK = RET_HEADS * RET_DK
RET_HV = RET_HEADS * RET_DV
RET_CHUNK = 256
ROPE_BASE = 10000.0
RET_COL_TILE = 256


def _ret_in_kernel(x_ref, pos_ref, freq_ref, w_ref, qkv_ref, gate_ref):
    tn = RET_COL_TILE
    n_qk = 2 * RET_HK // tn
    n_qkv = n_qk + RET_HV // tn
    half = RET_DK // 2
    x = x_ref[...].astype(BF16)
    ang = pos_ref[...].astype(F32) * freq_ref[...]
    c, s = jnp.cos(ang), jnp.sin(ang)
    for j in range(w_ref.shape[1] // tn):
        y = jnp.dot(x, w_ref[:, j * tn:(j + 1) * tn], preferred_element_type=F32)
        if j < n_qk:
            x1, x2 = y[:, :half], y[:, half:]
            scale = 1.0 if j < n_qk // 2 else RET_DK ** -0.5
            qkv_ref[:, j * tn:j * tn + half] = ((x1 * c - x2 * s) * scale).astype(BF16)
            qkv_ref[:, j * tn + half:(j + 1) * tn] = ((x1 * s + x2 * c) * scale).astype(BF16)
        elif j < n_qkv:
            qkv_ref[:, j * tn:(j + 1) * tn] = y.astype(BF16)
        else:
            gate_ref[:, (j - n_qkv) * tn:(j - n_qkv + 1) * tn] = y


def _ret_in(x, pos, w_in):
    t, d = x.shape
    tm = ROW_TILE
    n_all = w_in.shape[1]
    n_qkv = 2 * RET_HK + RET_HV
    half = RET_DK // 2
    freq = (ROPE_BASE ** (-jnp.arange(0, RET_DK, 2, dtype=F32) / RET_DK)).reshape(1, half)
    return pl.pallas_call(
        _ret_in_kernel,
        grid=(t // tm,),
        in_specs=[pl.BlockSpec((tm, d), lambda i: (i, 0)),
                  pl.BlockSpec((tm, 1), lambda i: (i, 0)),
                  pl.BlockSpec((1, half), lambda i: (0, 0)),
                  pl.BlockSpec((d, n_all), lambda i: (0, 0))],
        out_specs=[pl.BlockSpec((tm, n_qkv), lambda i: (i, 0)),
                   pl.BlockSpec((tm, n_all - n_qkv), lambda i: (i, 0))],
        out_shape=[jax.ShapeDtypeStruct((t, n_qkv), BF16),
                   jax.ShapeDtypeStruct((t, n_all - n_qkv), F32)],
        compiler_params=pltpu.CompilerParams(dimension_semantics=("arbitrary",),
                                             vmem_limit_bytes=VMEM_LIMIT_BYTES),
        name="ret_in",
    )(x, pos.reshape(t, 1), freq, w_in.astype(BF16))


def _ret_scan_kernel(q_ref, k_ref, v_ref, gate_ref, dec_ref, xi_ref, zeta_ref, gc_ref, gng_ref, gnb_ref,
                     o_ref, state_ref, y_ref):
    c = RET_CHUNK
    n_chunks = q_ref.shape[0] // c

    def chunk_out(i, d):
        rows = pl.ds(pl.multiple_of(i * c, c), c)
        qi, ki, vi = q_ref[rows, :], k_ref[rows, :], v_ref[rows, :]
        sc = lax.dot_general(qi, ki, (((1,), (1,)), ((), ())), preferred_element_type=F32)
        inner = jnp.dot((sc * dec_ref[d]).astype(BF16), vi, preferred_element_type=F32)
        cross = jnp.dot(qi, state_ref[...].astype(BF16), preferred_element_type=F32) * xi_ref[d]
        kz = (ki.astype(F32) * zeta_ref[d]).astype(BF16)
        upd = lax.dot_general(kz, vi, (((0,), (0,)), ((), ())), preferred_element_type=F32)
        state_ref[...] = gc_ref[d] * state_ref[...] + upd
        return rows, inner + cross

    state_ref[...] = jnp.zeros_like(state_ref)

    def fwd(i, carry):
        rows, y = chunk_out(i, 0)
        y_ref[rows, :] = y
        return carry

    lax.fori_loop(0, n_chunks, fwd, 0)
    state_ref[...] = jnp.zeros_like(state_ref)

    def bwd(n, carry):
        rows, y = chunk_out(n_chunks - 1 - n, 1)
        y = y + y_ref[rows, :]
        mu = jnp.mean(y, axis=-1, keepdims=True)
        dlt = y - mu
        var = jnp.mean(dlt * dlt, axis=-1, keepdims=True)
        yn = dlt * lax.rsqrt(var + LN_EPS) * gng_ref[...] + gnb_ref[...]
        g = gate_ref[rows, :]
        o_ref[rows, :] = (g * (1.0 / (1.0 + jnp.exp(-g))) * yn).astype(BF16)
        return carry

    lax.fori_loop(0, n_chunks, bwd, 0)


def _ret_scan(qkv, gate, log1m_decay, gn_g, gn_b, batch, seq):
    c = RET_CHUNK
    log_gamma = jnp.log1p(-jnp.exp(log1m_decay.astype(F32)))
    idx = jnp.arange(c, dtype=F32)
    diff = idx[:, None] - idx[None, :]
    lg = log_gamma[:, :, None, None]
    dec_f = jnp.where(diff >= 0, jnp.exp(jnp.maximum(diff, 0.0) * lg[0]), 0.0)
    dec_b = jnp.where(diff < 0, jnp.exp(jnp.maximum(-diff, 0.0) * lg[1]), 0.0)
    dec = jnp.stack([dec_f, dec_b], axis=1)
    lgc = log_gamma[:, :, None]
    xi = jnp.stack([jnp.exp((idx + 1.0) * lgc[0]), jnp.exp((c - idx) * lgc[1])], axis=1)[..., None]
    zeta = jnp.stack([jnp.exp((c - 1.0 - idx) * lgc[0]), jnp.exp(idx * lgc[1])], axis=1)[..., None]
    gc = jnp.exp(c * log_gamma).T.reshape(RET_HEADS, 2, 1, 1)
    t = batch * seq
    kq = RET_HK // RET_DK
    kv = 2 * RET_HK // RET_DV
    head4 = lambda b, h: (h, 0, 0, 0)
    return pl.pallas_call(
        _ret_scan_kernel,
        grid=(batch, RET_HEADS),
        in_specs=[pl.BlockSpec((seq, RET_DK), lambda b, h: (b, h)),
                  pl.BlockSpec((seq, RET_DK), lambda b, h: (b, kq + h)),
                  pl.BlockSpec((seq, RET_DV), lambda b, h: (b, kv + h)),
                  pl.BlockSpec((seq, RET_DV), lambda b, h: (b, h)),
                  pl.BlockSpec((None, 2, c, c), head4),
                  pl.BlockSpec((None, 2, c, 1), head4),
                  pl.BlockSpec((None, 2, c, 1), head4),
                  pl.BlockSpec((None, 2, 1, 1), head4),
                  pl.BlockSpec((1, RET_DV), lambda b, h: (0, h)),
                  pl.BlockSpec((1, RET_DV), lambda b, h: (0, h))],
        out_specs=pl.BlockSpec((seq, RET_DV), lambda b, h: (b, h)),
        out_shape=jax.ShapeDtypeStruct((t, RET_HV), BF16),
        scratch_shapes=[pltpu.VMEM((RET_DK, RET_DV), F32), pltpu.VMEM((seq, RET_DV), F32)],
        compiler_params=pltpu.CompilerParams(dimension_semantics=("arbitrary", "arbitrary"),
                                             vmem_limit_bytes=VMEM_LIMIT_BYTES),
        name="ret_scan",
    )(qkv, qkv, qkv, gate, dec, xi, zeta, gc, gn_g.reshape(1, RET_HV), gn_b.reshape(1, RET_HV))


def _retention_block(h, pos, w_in, log1m_decay, gn_g, gn_b, w_out, ln_g, ln_b, batch, seq):
    qkv, gate = _ret_in(h, pos, w_in)
    gated = _ret_scan(qkv, gate, log1m_decay, gn_g, gn_b, batch, seq)
    return _proj_ln(gated, w_out, h, ln_g, ln_b)


MLA_HEADS = 8
MLA_NOPE = 128
MLA_ROPE = 64
MLA_VDIM = 128
MLA_Q_RANK = 384
MLA_KV_RANK = 256
MLA_QK_PAD = 256
MLA_Q_TILE = 256


def _rms(x, g):
    ms = jnp.mean(x * x, axis=-1, keepdims=True)
    return x * lax.rsqrt(ms + LN_EPS) * g


def _mla_in_kernel(x_ref, pos_ref, freq_ref, win_ref, qg_ref, kvg_ref, wuq_ref, wukv_ref, q_ref, k_ref, v_ref):
    c = jnp.dot(x_ref[...].astype(BF16), win_ref[...], preferred_element_type=F32)
    lane = lax.broadcasted_iota(I32, (1, LANES), 1)
    half = MLA_ROPE // 2
    ang = pos_ref[...].astype(F32) * freq_ref[...]
    cosv = jnp.where(lane < MLA_ROPE, jnp.cos(ang), 0.0)
    sinv = jnp.sin(ang)
    sinv = jnp.where(lane < half, -sinv, jnp.where(lane < MLA_ROPE, sinv, 0.0))

    def rope(y):
        swapped = pltpu.roll(y, half, 1) + pltpu.roll(y, LANES - half, 1)
        return y * cosv + swapped * sinv

    cq = _rms(c[:, :MLA_Q_RANK], qg_ref[...]).astype(BF16)
    ckv = _rms(c[:, MLA_Q_RANK:MLA_Q_RANK + MLA_KV_RANK], kvg_ref[...]).astype(BF16)
    k_rope = rope(c[:, MLA_Q_RANK + MLA_KV_RANK:]).astype(BF16)
    q = jnp.dot(cq, wuq_ref[...], preferred_element_type=F32)
    kv = jnp.dot(ckv, wukv_ref[...], preferred_element_type=F32)
    for hd in range(MLA_HEADS):
        o = hd * MLA_QK_PAD
        q_ref[:, o:o + MLA_NOPE] = q[:, o:o + MLA_NOPE].astype(BF16)
        q_ref[:, o + MLA_NOPE:o + MLA_QK_PAD] = rope(q[:, o + MLA_NOPE:o + MLA_QK_PAD]).astype(BF16)
        k_ref[:, o:o + MLA_NOPE] = kv[:, o:o + MLA_NOPE].astype(BF16)
        k_ref[:, o + MLA_NOPE:o + MLA_QK_PAD] = k_rope
        v_ref[:, hd * MLA_VDIM:(hd + 1) * MLA_VDIM] = kv[:, o + MLA_NOPE:o + MLA_QK_PAD].astype(BF16)


def _mla_in(x, pos, w_in, q_norm_g, kv_norm_g, w_uq, w_ukv):
    t, d = x.shape
    tm = ROW_TILE
    pad = MLA_QK_PAD - MLA_NOPE - MLA_ROPE
    win = jnp.pad(w_in, ((0, 0), (0, pad))).astype(BF16)
    wuq = jnp.pad(w_uq.reshape(MLA_Q_RANK, MLA_HEADS, MLA_NOPE + MLA_ROPE), ((0, 0), (0, 0), (0, pad)))
    wuq = wuq.reshape(MLA_Q_RANK, MLA_HEADS * MLA_QK_PAD).astype(BF16)
    wukv = w_ukv.astype(BF16)
    f = ROPE_BASE ** (-jnp.arange(0, MLA_ROPE, 2, dtype=F32) / MLA_ROPE)
    freq = jnp.concatenate([f, f, jnp.zeros((LANES - MLA_ROPE,), F32)]).reshape(1, LANES)
    full = lambda a: pl.BlockSpec(a.shape, lambda i: (0,) * a.ndim)
    qg = q_norm_g.reshape(1, -1)
    kvg = kv_norm_g.reshape(1, -1)
    wide = MLA_HEADS * MLA_QK_PAD
    return pl.pallas_call(
        _mla_in_kernel,
        grid=(t // tm,),
        in_specs=[pl.BlockSpec((tm, d), lambda i: (i, 0)), pl.BlockSpec((tm, 1), lambda i: (i, 0)),
                  full(freq), full(win), full(qg), full(kvg), full(wuq), full(wukv)],
        out_specs=[pl.BlockSpec((tm, wide), lambda i: (i, 0)), pl.BlockSpec((tm, wide), lambda i: (i, 0)),
                   pl.BlockSpec((tm, MLA_HEADS * MLA_VDIM), lambda i: (i, 0))],
        out_shape=[jax.ShapeDtypeStruct((t, wide), BF16), jax.ShapeDtypeStruct((t, wide), BF16),
                   jax.ShapeDtypeStruct((t, MLA_HEADS * MLA_VDIM), BF16)],
        compiler_params=pltpu.CompilerParams(dimension_semantics=("arbitrary",),
                                             vmem_limit_bytes=VMEM_LIMIT_BYTES),
        name="mla_in",
    )(x, pos.reshape(t, 1), freq, win, qg, kvg, wuq, wukv)


def _mla_attn_kernel(q_ref, k_ref, v_ref, o_ref):
    s = lax.dot_general(q_ref[...], k_ref[...], (((1,), (1,)), ((), ())), preferred_element_type=F32)
    s = s * ((MLA_NOPE + MLA_ROPE) ** -0.5)
    p = jnp.exp(s - jnp.max(s, axis=-1, keepdims=True))
    denom = jnp.sum(p, axis=-1, keepdims=True)
    o = jnp.dot(p.astype(BF16), v_ref[...], preferred_element_type=F32)
    o_ref[...] = (o / denom).astype(BF16)


def _mla_attn(q, k, v, batch, seq):
    t = batch * seq
    tq = MLA_Q_TILE
    nq = seq // tq
    return pl.pallas_call(
        _mla_attn_kernel,
        grid=(batch, MLA_HEADS, nq),
        in_specs=[pl.BlockSpec((tq, MLA_QK_PAD), lambda b, h, i: (b * nq + i, h)),
                  pl.BlockSpec((seq, MLA_QK_PAD), lambda b, h, i: (b, h)),
                  pl.BlockSpec((seq, MLA_VDIM), lambda b, h, i: (b, h))],
        out_specs=pl.BlockSpec((tq, MLA_VDIM), lambda b, h, i: (b * nq + i, h)),
        out_shape=jax.ShapeDtypeStruct((t, MLA_HEADS * MLA_VDIM), BF16),
        compiler_params=pltpu.CompilerParams(dimension_semantics=("arbitrary", "arbitrary", "arbitrary"),
                                             vmem_limit_bytes=VMEM_LIMIT_BYTES),
        name="mla_attn",
    )(q, k, v)


def _mla_block(h, pos, w_in, q_norm_g, kv_norm_g, w_uq, w_ukv, w_out, ln_g, ln_b, batch, seq):
    q, k, v = _mla_in(h, pos, w_in, q_norm_g, kv_norm_g, w_uq, w_ukv)
    o = _mla_attn(q, k, v, batch, seq)
    return _proj_ln(o, w_out, h, ln_g, ln_b)


def kernel(x, positions, ret_w_in, ret_log1m_decay, ret_gn_g, ret_gn_b, ret_w_out, mla_w_in, mla_q_norm,
           mla_kv_norm, mla_w_uq, mla_w_ukv, mla_w_out, peer_w_q, peer_sub_keys, peer_u, peer_v, ln_mix_g,
           ln_mix_b, ln_ffn_g, ln_ffn_b):
    b, s, d = x.shape
    t = b * s
    h = x.reshape(t, d)
    pos = positions.reshape(t)
    for i in range(DEPTH):
        j = i // 2
        if i % 2 == 0:
            h, h8 = _retention_block(h, pos, ret_w_in[j], ret_log1m_decay[j], ret_gn_g[j], ret_gn_b[j],
                                     ret_w_out[j], ln_mix_g[i], ln_mix_b[i], b, s)
        else:
            h, h8 = _mla_block(h, pos, mla_w_in[j], mla_q_norm[j], mla_kv_norm[j], mla_w_uq[j], mla_w_ukv[j],
                               mla_w_out[j], ln_mix_g[i], ln_mix_b[i], b, s)
        h = _peer_block(h, h8, peer_w_q[i], peer_sub_keys[i], peer_u[i], peer_v[i], ln_ffn_g[i], ln_ffn_b[i])
    return h.reshape(b, s, d)
```

```python
import functools

import jax
import jax.numpy as jnp
from jax import lax
from jax.experimental import pallas as pl
from jax.experimental.pallas import tpu as pltpu

F32 = jnp.float32
BF16 = jnp.bfloat16
I32 = jnp.int32
U32 = jnp.uint32

D_MODEL = 1024
DEPTH = 2
DN_ALPHA = (2.0 * DEPTH) ** 0.25
LN_EPS = 1e-5

PEER_HEADS = 8
PEER_NKEYS = 128
PEER_HALF = 128
PEER_TOPK = 16
PEER_SLOTS = PEER_HEADS * PEER_TOPK
PEER_EXPERTS = PEER_NKEYS * PEER_NKEYS
PEER_HALF_EXPERTS = PEER_EXPERTS // 2

SUBLANES = 8
LANES = 128
VMEM_LIMIT_BYTES = 56 * 1024 * 1024

ROW_TILE = 512
SEL_TOKENS = 256
GATHER_TOKENS = LANES


def _top_rows(st, ids, k, payload=None):
    vals, sel_ids, sel_pay = [], [], []
    big = jnp.int32(2 ** 30)
    for _ in range(k):
        m = jnp.max(st, axis=0, keepdims=True)
        sel = jnp.min(jnp.where(st == m, ids, big), axis=0, keepdims=True)
        hit = ids == sel
        vals.append(m)
        sel_ids.append(sel)
        if payload is not None:
            sel_pay.append(jnp.max(jnp.where(hit, payload, -1), axis=0, keepdims=True))
        st = jnp.where(hit, -jnp.inf, st)
    return vals, sel_ids, sel_pay


def _peer_select_kernel(h_ref, wq_ref, keys_ref, row_ref, up_ref, gate_ref, s_buf, i_buf, e_buf, g_buf):
    tb = h_ref.shape[0]
    q = jnp.dot(h_ref[...].astype(BF16), wq_ref[...], preferred_element_type=F32)
    key_ids = lax.broadcasted_iota(I32, (PEER_NKEYS, tb), 0)
    for head in range(PEER_HEADS):
        for c in range(2):
            col = (head * 2 + c) * PEER_HALF
            qhc = q[:, col:col + PEER_HALF].astype(BF16)
            st = lax.dot_general(keys_ref[head, c], qhc, (((1,), (1,)), ((), ())),
                                 preferred_element_type=F32)
            vals, ids, _ = _top_rows(st, key_ids, PEER_TOPK)
            for a in range(PEER_TOPK):
                s_buf[c, pl.ds(a, 1), :] = vals[a]
                i_buf[c, pl.ds(a, 1), :] = ids[a]
        cand_s, cand_pos, cand_e = [], [], []
        for b in range(PEER_TOPK):
            na = PEER_TOPK if b == 0 else SUBLANES
            cand_s.append(s_buf[0, :na, :] + s_buf[1, pl.ds(b, 1), :])
            cand_pos.append(lax.broadcasted_iota(I32, (na, tb), 0) * PEER_TOPK + b)
            cand_e.append(i_buf[0, :na, :] * PEER_NKEYS + i_buf[1, pl.ds(b, 1), :])
        cand_s = jnp.concatenate(cand_s, axis=0)
        cand_pos = jnp.concatenate(cand_pos, axis=0)
        cand_e = jnp.concatenate(cand_e, axis=0)
        top_s, _, top_e = _top_rows(cand_s, cand_pos, PEER_TOPK, payload=cand_e)
        ex = [jnp.exp(t - top_s[0]) for t in top_s]
        denom = functools.reduce(lambda x, y: x + y, ex)
        inv = 1.0 / denom
        for k in range(PEER_TOPK):
            slot = head * PEER_TOPK + k
            e_buf[pl.ds(slot, 1), :] = top_e[k]
            g_buf[pl.ds(slot, 1), :] = ex[k] * inv
    for s in range(tb // LANES):
        cols = slice(s * LANES, (s + 1) * LANES)
        e = e_buf[:, cols]
        row_ref[s] = (e & (PEER_HALF_EXPERTS - 1)) * SUBLANES
        up_ref[cols, :] = jnp.where(e >= PEER_HALF_EXPERTS, 1.0, 0.0).T
        gate_ref[cols, :] = g_buf[:, cols].T


def _peer_select(h, w_q, sub_keys):
    t = h.shape[0]
    tb = SEL_TOKENS
    wq = w_q.astype(BF16)
    keys = sub_keys.astype(BF16)
    tok = pl.BlockSpec((tb, PEER_SLOTS), lambda i: (i, 0))
    return pl.pallas_call(
        _peer_select_kernel,
        grid=(t // tb,),
        in_specs=[pl.BlockSpec((tb, D_MODEL), lambda i: (i, 0)),
                  pl.BlockSpec(wq.shape, lambda i: (0, 0)),
                  pl.BlockSpec(keys.shape, lambda i: (0, 0, 0, 0))],
        out_specs=[pl.BlockSpec((tb // LANES, PEER_SLOTS, LANES), lambda i: (i, 0, 0)), tok, tok],
        out_shape=[jax.ShapeDtypeStruct((t // LANES, PEER_SLOTS, LANES), I32),
                   jax.ShapeDtypeStruct((t, PEER_SLOTS), F32),
                   jax.ShapeDtypeStruct((t, PEER_SLOTS), F32)],
        scratch_shapes=[pltpu.VMEM((2, PEER_TOPK, tb), F32), pltpu.VMEM((2, PEER_TOPK, tb), I32),
                        pltpu.VMEM((PEER_SLOTS, tb), I32), pltpu.VMEM((PEER_SLOTS, tb), F32)],
        compiler_params=pltpu.CompilerParams(dimension_semantics=("arbitrary",),
                                             vmem_limit_bytes=VMEM_LIMIT_BYTES),
        name="peer_select",
    )(h, wq, keys)


HIGH_HALF = 0xFFFF0000
PACK_ROWS = 512


def _pack_table_kernel(lo_ref, hi_ref, o_ref):
    lo = pltpu.bitcast(lo_ref[...].astype(BF16).astype(F32), U32) >> 16
    hi = pltpu.bitcast(hi_ref[...].astype(BF16).astype(F32), U32) & jnp.uint32(HIGH_HALF)
    packed = lo | hi
    rows = packed.shape[0]
    for c in range(D_MODEL // LANES):
        o_ref[pl.ds(c, rows, stride=SUBLANES), :] = packed[:, c * LANES:(c + 1) * LANES]


def _pack_table(w, layer):
    r = PACK_ROWS
    steps = PEER_HALF_EXPERTS // r
    return pl.pallas_call(
        _pack_table_kernel,
        grid=(steps,),
        in_specs=[pl.BlockSpec((None, r, D_MODEL), lambda i: (layer, i, 0)),
                  pl.BlockSpec((None, r, D_MODEL), lambda i: (layer, i + steps, 0))],
        out_specs=pl.BlockSpec((r * SUBLANES, LANES), lambda i: (i, 0)),
        out_shape=jax.ShapeDtypeStruct((PEER_HALF_EXPERTS * SUBLANES, LANES), U32),
        compiler_params=pltpu.CompilerParams(dimension_semantics=("arbitrary",),
                                             vmem_limit_bytes=VMEM_LIMIT_BYTES),
        name="pack_table",
    )(w, w)


def _packed_rowsum(ws, sub):
    def add(a, b):
        return pltpu.bitcast(pltpu.bitcast(a, BF16) + pltpu.bitcast(b, BF16), U32)

    m4 = sub < 4
    lvl1 = []
    for j in range(4):
        x, y = ws[j], ws[j + 4]
        lvl1.append(add(jnp.where(m4, x, y), pltpu.roll(jnp.where(m4, y, x), 4, 0)))
    m2 = (sub & 2) == 0
    lvl2 = []
    for j in range(2):
        x, y = lvl1[j], lvl1[j + 2]
        lvl2.append(add(jnp.where(m2, x, pltpu.roll(y, 2, 0)), jnp.where(m2, pltpu.roll(x, 6, 0), y)))
    m1 = (sub & 1) == 0
    x, y = lvl2[0], lvl2[1]
    return add(jnp.where(m1, x, pltpu.roll(y, 1, 0)), jnp.where(m1, pltpu.roll(x, 7, 0), y))


def _peer_u_kernel(row_ref, x_ref, up_ref, gate_ref, tbl_ref, c_ref):
    tt = gate_ref.shape[0]
    n_chain = 2
    sub = lax.broadcasted_iota(I32, (SUBLANES, LANES), 0)
    lane = lax.broadcasted_iota(I32, (SUBLANES, LANES), 1)
    high = jnp.uint32(HIGH_HALF)

    def group(g):
        t0 = g * SUBLANES
        xs = []
        for j in range(SUBLANES):
            xrows = pl.ds((t0 + j) * SUBLANES, SUBLANES)
            bits = pltpu.bitcast(x_ref[xrows, :].astype(BF16).astype(F32), U32)
            xs.append(pltpu.bitcast(bits | (bits >> 16), BF16))
        lo = [jnp.zeros((SUBLANES, LANES), F32) for _ in range(n_chain)]
        hi = [jnp.zeros((SUBLANES, LANES), F32) for _ in range(n_chain)]
        for p in range(PEER_SLOTS):
            ws = []
            for j in range(SUBLANES):
                row8 = pl.multiple_of(row_ref[t0 + (p * tt + j)], SUBLANES)
                w = pltpu.bitcast(tbl_ref[pl.ds(row8, SUBLANES), :], BF16)
                ws.append(pltpu.bitcast(w * xs[j], U32))
            r = _packed_rowsum(ws, sub)
            here = lane == p
            c = p % n_chain
            lo[c] = jnp.where(here, jnp.sum(pltpu.bitcast(r << 16, F32), axis=1, keepdims=True), lo[c])
            hi[c] = jnp.where(here, jnp.sum(pltpu.bitcast(r & high, F32), axis=1, keepdims=True), hi[c])
        rows = pl.ds(t0, SUBLANES)
        hdot = jnp.where(up_ref[rows, :] > 0.5, hi[0] + hi[1], lo[0] + lo[1])
        act = 0.5 * hdot * (1.0 + lax.erf(hdot * (2.0 ** -0.5)))
        c_ref[rows, :] = act * gate_ref[rows, :]

    for g in range(tt // SUBLANES):
        group(g)


V_SLOT_RUN = 4


def _peer_v_kernel(row_ref, up_ref, c_ref, tbl_ref, o_ref):
    tt = c_ref.shape[0]
    lane = lax.broadcasted_iota(I32, (SUBLANES, LANES), 1)
    high = jnp.uint32(HIGH_HALF)
    full = (SUBLANES, LANES)

    def group(g):
        t0 = g * SUBLANES
        rows = pl.ds(t0, SUBLANES)
        uppers = up_ref[rows, :]
        coefs = c_ref[rows, :]
        acc_lo = [jnp.zeros(full, F32) for _ in range(SUBLANES)]
        acc_hi = [jnp.zeros(full, F32) for _ in range(SUBLANES)]
        for p0 in range(0, PEER_SLOTS, V_SLOT_RUN):
            prods = [[] for _ in range(SUBLANES)]
            for p in range(p0, p0 + V_SLOT_RUN):
                here = lane == p
                coef = jnp.broadcast_to(jnp.sum(jnp.where(here, coefs, 0.0), axis=1, keepdims=True), full)
                upper = jnp.broadcast_to(jnp.sum(jnp.where(here, uppers, 0.0), axis=1, keepdims=True), full)
                cbits = pltpu.bitcast(coef.astype(BF16).astype(F32), U32)
                cpair = jnp.where(upper > 0.5, cbits, cbits >> 16)
                for j in range(SUBLANES):
                    row8 = pl.multiple_of(row_ref[t0 + (p * tt + j)], SUBLANES)
                    w = pltpu.bitcast(tbl_ref[pl.ds(row8, SUBLANES), :], BF16)
                    cj = pltpu.bitcast(jnp.broadcast_to(cpair[j:j + 1, :], full), BF16)
                    prods[j].append(w * cj)
            for j in range(SUBLANES):
                pr = prods[j]
                while len(pr) > 1:
                    pr = [pr[i] + pr[i + 1] for i in range(0, len(pr), 2)]
                s = pltpu.bitcast(pr[0], U32)
                acc_lo[j] = acc_lo[j] + pltpu.bitcast(s << 16, F32)
                acc_hi[j] = acc_hi[j] + pltpu.bitcast(s & high, F32)
        for j in range(SUBLANES):
            o_ref[pl.ds((t0 + j) * SUBLANES, SUBLANES), :] = acc_lo[j] + acc_hi[j]

    for g in range(tt // SUBLANES):
        group(g)


def _peer_retrieve(x8, row8, upper, gate, u_tbl, v_tbl):
    t = gate.shape[0]
    tt = GATHER_TOKENS
    smem_spec = pl.BlockSpec((tt * PEER_SLOTS,), lambda i: (i,), memory_space=pltpu.SMEM,
                             pipeline_mode=pl.Buffered(1))
    slots = pl.BlockSpec((tt, PEER_SLOTS), lambda i: (i, 0))
    rows8 = pl.BlockSpec((tt * SUBLANES, LANES), lambda i: (i, 0))
    table_spec = pl.BlockSpec(memory_space=pltpu.VMEM)
    params = pltpu.CompilerParams(dimension_semantics=("arbitrary",), vmem_limit_bytes=VMEM_LIMIT_BYTES)
    row8 = row8.reshape(-1)
    coef = pl.pallas_call(
        _peer_u_kernel,
        grid=(t // tt,),
        in_specs=[smem_spec, rows8, slots, slots, table_spec],
        out_specs=slots,
        out_shape=jax.ShapeDtypeStruct((t, PEER_SLOTS), F32),
        compiler_params=params,
        name="peer_u",
    )(row8, x8, upper, gate, u_tbl)
    return pl.pallas_call(
        _peer_v_kernel,
        grid=(t // tt,),
        in_specs=[smem_spec, slots, slots, table_spec],
        out_specs=rows8,
        out_shape=jax.ShapeDtypeStruct((t * SUBLANES, LANES), F32),
        compiler_params=params,
        name="peer_v",
    )(row8, upper, coef, v_tbl)


def _layer_norm(y, g, b):
    mu = jnp.mean(y, axis=-1, keepdims=True)
    d = y - mu
    var = jnp.mean(d * d, axis=-1, keepdims=True)
    return d * lax.rsqrt(var + LN_EPS) * g + b


def _store_tiles(o8_ref, out):
    tm = out.shape[0]
    for c in range(D_MODEL // LANES):
        o8_ref[pl.ds(c, tm, stride=SUBLANES), :] = out[:, c * LANES:(c + 1) * LANES]


def _add_ln_kernel(h_ref, f8_ref, g_ref, b_ref, o_ref):
    tm = h_ref.shape[0]
    f = jnp.concatenate([f8_ref[pl.ds(c, tm, stride=SUBLANES), :] for c in range(D_MODEL // LANES)], axis=1)
    o_ref[...] = _layer_norm(DN_ALPHA * h_ref[...] + f, g_ref[...], b_ref[...])


def _add_ln(h, f8, g, b):
    t, d = h.shape
    tb = ROW_TILE
    row = pl.BlockSpec((tb, d), lambda i: (i, 0))
    vec = pl.BlockSpec((1, d), lambda i: (0, 0))
    return pl.pallas_call(
        _add_ln_kernel,
        grid=(t // tb,),
        in_specs=[row, pl.BlockSpec((tb * SUBLANES, LANES), lambda i: (i, 0)), vec, vec],
        out_specs=row,
        out_shape=jax.ShapeDtypeStruct((t, d), F32),
        compiler_params=pltpu.CompilerParams(dimension_semantics=("arbitrary",)),
        name="add_ln",
    )(h, f8, g.reshape(1, d), b.reshape(1, d))


def _peer_block(h, h8, w_q, sub_keys, u_all, v_all, layer, ln_g, ln_b):
    row8, upper, gate = _peer_select(h, w_q, sub_keys)
    f8 = _peer_retrieve(h8, row8, upper, gate, _pack_table(u_all, layer), _pack_table(v_all, layer))
    return _add_ln(h, f8, ln_g, ln_b)


def _proj_ln_kernel(a_ref, w_ref, h_ref, g_ref, b_ref, o_ref, o8_ref):
    y = DN_ALPHA * h_ref[...] + jnp.dot(a_ref[...], w_ref[...], preferred_element_type=F32)
    out = _layer_norm(y, g_ref[...], b_ref[...])
    o_ref[...] = out
    _store_tiles(o8_ref, out)


def _proj_ln(a, w, h, g, b):
    t, k = a.shape
    d = w.shape[1]
    tm = ROW_TILE
    vec = pl.BlockSpec((1, d), lambda i: (0, 0))
    return pl.pallas_call(
        _proj_ln_kernel,
        grid=(t // tm,),
        in_specs=[pl.BlockSpec((tm, k), lambda i: (i, 0)), pl.BlockSpec((k, d), lambda i: (0, 0)),
                  pl.BlockSpec((tm, d), lambda i: (i, 0)), vec, vec],
        out_specs=[pl.BlockSpec((tm, d), lambda i: (i, 0)),
                   pl.BlockSpec((tm * SUBLANES, LANES), lambda i: (i, 0))],
        out_shape=[jax.ShapeDtypeStruct((t, d), F32), jax.ShapeDtypeStruct((t * SUBLANES, LANES), F32)],
        compiler_params=pltpu.CompilerParams(dimension_semantics=("arbitrary",),
                                             vmem_limit_bytes=VMEM_LIMIT_BYTES),
        name="proj_ln",
    )(a, w.astype(BF16), h, g.reshape(1, d), b.reshape(1, d))


RET_HEADS = 4
RET_DK = 256
RET_DV = 512
RET_HK = RET_HEADS * RET_DK
RET_HV = RET_HEADS * RET_DV
RET_CHUNK = 256
ROPE_BASE = 10000.0
RET_COL_TILE = 256


def _ret_in_kernel(x_ref, pos_ref, freq_ref, w_ref, qkv_ref, gate_ref):
    tn = RET_COL_TILE
    n_qk = 2 * RET_HK // tn
    n_qkv = n_qk + RET_HV // tn
    half = RET_DK // 2
    x = x_ref[...].astype(BF16)
    ang = pos_ref[...].astype(F32) * freq_ref[...]
    c, s = jnp.cos(ang), jnp.sin(ang)
    for j in range(w_ref.shape[1] // tn):
        y = jnp.dot(x, w_ref[:, j * tn:(j + 1) * tn], preferred_element_type=F32)
        if j < n_qk:
            x1, x2 = y[:, :half], y[:, half:]
            scale = 1.0 if j < n_qk // 2 else RET_DK ** -0.5
            qkv_ref[:, j * tn:j * tn + half] = ((x1 * c - x2 * s) * scale).astype(BF16)
            qkv_ref[:, j * tn + half:(j + 1) * tn] = ((x1 * s + x2 * c) * scale).astype(BF16)
        elif j < n_qkv:
            qkv_ref[:, j * tn:(j + 1) * tn] = y.astype(BF16)
        else:
            gate_ref[:, (j - n_qkv) * tn:(j - n_qkv + 1) * tn] = y


def _ret_in(x, pos, w_in):
    t, d = x.shape
    tm = ROW_TILE
    n_all = w_in.shape[1]
    n_qkv = 2 * RET_HK + RET_HV
    half = RET_DK // 2
    freq = (ROPE_BASE ** (-jnp.arange(0, RET_DK, 2, dtype=F32) / RET_DK)).reshape(1, half)
    return pl.pallas_call(
        _ret_in_kernel,
        grid=(t // tm,),
        in_specs=[pl.BlockSpec((tm, d), lambda i: (i, 0)),
                  pl.BlockSpec((tm, 1), lambda i: (i, 0)),
                  pl.BlockSpec((1, half), lambda i: (0, 0)),
                  pl.BlockSpec((d, n_all), lambda i: (0, 0))],
        out_specs=[pl.BlockSpec((tm, n_qkv), lambda i: (i, 0)),
                   pl.BlockSpec((tm, n_all - n_qkv), lambda i: (i, 0))],
        out_shape=[jax.ShapeDtypeStruct((t, n_qkv), BF16),
                   jax.ShapeDtypeStruct((t, n_all - n_qkv), F32)],
        compiler_params=pltpu.CompilerParams(dimension_semantics=("arbitrary",),
                                             vmem_limit_bytes=VMEM_LIMIT_BYTES),
        name="ret_in",
    )(x, pos.reshape(t, 1), freq, w_in.astype(BF16))


def _ret_scan_kernel(q_ref, k_ref, v_ref, gate_ref, dec_ref, xi_ref, zeta_ref, gc_ref, gng_ref, gnb_ref,
                     o_ref, state_ref, y_ref):
    c = RET_CHUNK
    n_chunks = q_ref.shape[0] // c

    def chunk_out(i, d):
        rows = pl.ds(pl.multiple_of(i * c, c), c)
        qi, ki, vi = q_ref[rows, :], k_ref[rows, :], v_ref[rows, :]
        sc = lax.dot_general(qi, ki, (((1,), (1,)), ((), ())), preferred_element_type=F32)
        inner = jnp.dot((sc * dec_ref[d]).astype(BF16), vi, preferred_element_type=F32)
        cross = jnp.dot(qi, state_ref[...].astype(BF16), preferred_element_type=F32) * xi_ref[d]
        kz = (ki.astype(F32) * zeta_ref[d]).astype(BF16)
        upd = lax.dot_general(kz, vi, (((0,), (0,)), ((), ())), preferred_element_type=F32)
        state_ref[...] = gc_ref[d] * state_ref[...] + upd
        return rows, inner + cross

    state_ref[...] = jnp.zeros_like(state_ref)

    def fwd(i, carry):
        rows, y = chunk_out(i, 0)
        y_ref[rows, :] = y
        return carry

    lax.fori_loop(0, n_chunks, fwd, 0)
    state_ref[...] = jnp.zeros_like(state_ref)

    def bwd(n, carry):
        rows, y = chunk_out(n_chunks - 1 - n, 1)
        y = y + y_ref[rows, :]
        mu = jnp.mean(y, axis=-1, keepdims=True)
        dlt = y - mu
        var = jnp.mean(dlt * dlt, axis=-1, keepdims=True)
        yn = dlt * lax.rsqrt(var + LN_EPS) * gng_ref[...] + gnb_ref[...]
        g = gate_ref[rows, :]
        o_ref[rows, :] = (g * (1.0 / (1.0 + jnp.exp(-g))) * yn).astype(BF16)
        return carry

    lax.fori_loop(0, n_chunks, bwd, 0)


def _ret_scan(qkv, gate, log1m_decay, gn_g, gn_b, batch, seq):
    c = RET_CHUNK
    log_gamma = jnp.log1p(-jnp.exp(log1m_decay.astype(F32)))
    idx = jnp.arange(c, dtype=F32)
    diff = idx[:, None] - idx[None, :]
    lg = log_gamma[:, :, None, None]
    dec_f = jnp.where(diff >= 0, jnp.exp(jnp.maximum(diff, 0.0) * lg[0]), 0.0)
    dec_b = jnp.where(diff < 0, jnp.exp(jnp.maximum(-diff, 0.0) * lg[1]), 0.0)
    dec = jnp.stack([dec_f, dec_b], axis=1)
    lgc = log_gamma[:, :, None]
    xi = jnp.stack([jnp.exp((idx + 1.0) * lgc[0]), jnp.exp((c - idx) * lgc[1])], axis=1)[..., None]
    zeta = jnp.stack([jnp.exp((c - 1.0 - idx) * lgc[0]), jnp.exp(idx * lgc[1])], axis=1)[..., None]
    gc = jnp.exp(c * log_gamma).T.reshape(RET_HEADS, 2, 1, 1)
    t = batch * seq
    kq = RET_HK // RET_DK
    kv = 2 * RET_HK // RET_DV
    head4 = lambda b, h: (h, 0, 0, 0)
    return pl.pallas_call(
        _ret_scan_kernel,
        grid=(batch, RET_HEADS),
        in_specs=[pl.BlockSpec((seq, RET_DK), lambda b, h: (b, h)),
                  pl.BlockSpec((seq, RET_DK), lambda b, h: (b, kq + h)),
                  pl.BlockSpec((seq, RET_DV), lambda b, h: (b, kv + h)),
                  pl.BlockSpec((seq, RET_DV), lambda b, h: (b, h)),
                  pl.BlockSpec((None, 2, c, c), head4),
                  pl.BlockSpec((None, 2, c, 1), head4),
                  pl.BlockSpec((None, 2, c, 1), head4),
                  pl.BlockSpec((None, 2, 1, 1), head4),
                  pl.BlockSpec((1, RET_DV), lambda b, h: (0, h)),
                  pl.BlockSpec((1, RET_DV), lambda b, h: (0, h))],
        out_specs=pl.BlockSpec((seq, RET_DV), lambda b, h: (b, h)),
        out_shape=jax.ShapeDtypeStruct((t, RET_HV), BF16),
        scratch_shapes=[pltpu.VMEM((RET_DK, RET_DV), F32), pltpu.VMEM((seq, RET_DV), F32)],
        compiler_params=pltpu.CompilerParams(dimension_semantics=("arbitrary", "arbitrary"),
                                             vmem_limit_bytes=VMEM_LIMIT_BYTES),
        name="ret_scan",
    )(qkv, qkv, qkv, gate, dec, xi, zeta, gc, gn_g.reshape(1, RET_HV), gn_b.reshape(1, RET_HV))


def _retention_block(h, pos, w_in, log1m_decay, gn_g, gn_b, w_out, ln_g, ln_b, batch, seq):
    qkv, gate = _ret_in(h, pos, w_in)
    gated = _ret_scan(qkv, gate, log1m_decay, gn_g, gn_b, batch, seq)
    return _proj_ln(gated, w_out, h, ln_g, ln_b)


MLA_HEADS = 8
MLA_NOPE = 128
MLA_ROPE = 64
MLA_VDIM = 128
MLA_Q_RANK = 384
MLA_KV_RANK = 256
MLA_QK_PAD = 256
MLA_Q_TILE = 256


def _rms(x, g):
    ms = jnp.mean(x * x, axis=-1, keepdims=True)
    return x * lax.rsqrt(ms + LN_EPS) * g


def _mla_in_kernel(x_ref, pos_ref, freq_ref, win_ref, qg_ref, kvg_ref, wuq_ref, wukv_ref, q_ref, k_ref, v_ref):
    c = jnp.dot(x_ref[...].astype(BF16), win_ref[...], preferred_element_type=F32)
    lane = lax.broadcasted_iota(I32, (1, LANES), 1)
    half = MLA_ROPE // 2
    ang = pos_ref[...].astype(F32) * freq_ref[...]
    cosv = jnp.where(lane < MLA_ROPE, jnp.cos(ang), 0.0)
    sinv = jnp.sin(ang)
    sinv = jnp.where(lane < half, -sinv, jnp.where(lane < MLA_ROPE, sinv, 0.0))

    def rope(y):
        swapped = pltpu.roll(y, half, 1) + pltpu.roll(y, LANES - half, 1)
        return y * cosv + swapped * sinv

    cq = _rms(c[:, :MLA_Q_RANK], qg_ref[...]).astype(BF16)
    ckv = _rms(c[:, MLA_Q_RANK:MLA_Q_RANK + MLA_KV_RANK], kvg_ref[...]).astype(BF16)
    k_rope = rope(c[:, MLA_Q_RANK + MLA_KV_RANK:]).astype(BF16)
    q = jnp.dot(cq, wuq_ref[...], preferred_element_type=F32)
    kv = jnp.dot(ckv, wukv_ref[...], preferred_element_type=F32)
    for hd in range(MLA_HEADS):
        o = hd * MLA_QK_PAD
        q_ref[:, o:o + MLA_NOPE] = q[:, o:o + MLA_NOPE].astype(BF16)
        q_ref[:, o + MLA_NOPE:o + MLA_QK_PAD] = rope(q[:, o + MLA_NOPE:o + MLA_QK_PAD]).astype(BF16)
        k_ref[:, o:o + MLA_NOPE] = kv[:, o:o + MLA_NOPE].astype(BF16)
        k_ref[:, o + MLA_NOPE:o + MLA_QK_PAD] = k_rope
        v_ref[:, hd * MLA_VDIM:(hd + 1) * MLA_VDIM] = kv[:, o + MLA_NOPE:o + MLA_QK_PAD].astype(BF16)


def _mla_in(x, pos, w_in, q_norm_g, kv_norm_g, w_uq, w_ukv):
    t, d = x.shape
    tm = ROW_TILE
    pad = MLA_QK_PAD - MLA_NOPE - MLA_ROPE
    win = jnp.pad(w_in, ((0, 0), (0, pad))).astype(BF16)
    wuq = jnp.pad(w_uq.reshape(MLA_Q_RANK, MLA_HEADS, MLA_NOPE + MLA_ROPE), ((0, 0), (0, 0), (0, pad)))
    wuq = wuq.reshape(MLA_Q_RANK, MLA_HEADS * MLA_QK_PAD).astype(BF16)
    wukv = w_ukv.astype(BF16)
    f = ROPE_BASE ** (-jnp.arange(0, MLA_ROPE, 2, dtype=F32) / MLA_ROPE)
    freq = jnp.concatenate([f, f, jnp.zeros((LANES - MLA_ROPE,), F32)]).reshape(1, LANES)
    full = lambda a: pl.BlockSpec(a.shape, lambda i: (0,) * a.ndim)
    qg = q_norm_g.reshape(1, -1)
    kvg = kv_norm_g.reshape(1, -1)
    wide = MLA_HEADS * MLA_QK_PAD
    return pl.pallas_call(
        _mla_in_kernel,
        grid=(t // tm,),
        in_specs=[pl.BlockSpec((tm, d), lambda i: (i, 0)), pl.BlockSpec((tm, 1), lambda i: (i, 0)),
                  full(freq), full(win), full(qg), full(kvg), full(wuq), full(wukv)],
        out_specs=[pl.BlockSpec((tm, wide), lambda i: (i, 0)), pl.BlockSpec((tm, wide), lambda i: (i, 0)),
                   pl.BlockSpec((tm, MLA_HEADS * MLA_VDIM), lambda i: (i, 0))],
        out_shape=[jax.ShapeDtypeStruct((t, wide), BF16), jax.ShapeDtypeStruct((t, wide), BF16),
                   jax.ShapeDtypeStruct((t, MLA_HEADS * MLA_VDIM), BF16)],
        compiler_params=pltpu.CompilerParams(dimension_semantics=("arbitrary",),
                                             vmem_limit_bytes=VMEM_LIMIT_BYTES),
        name="mla_in",
    )(x, pos.reshape(t, 1), freq, win, qg, kvg, wuq, wukv)


def _mla_attn_kernel(q_ref, k_ref, v_ref, o_ref):
    s = lax.dot_general(q_ref[...], k_ref[...], (((1,), (1,)), ((), ())), preferred_element_type=F32)
    s = s * ((MLA_NOPE + MLA_ROPE) ** -0.5)
    p = jnp.exp(s - jnp.max(s, axis=-1, keepdims=True))
    denom = jnp.sum(p, axis=-1, keepdims=True)
    o = jnp.dot(p.astype(BF16), v_ref[...], preferred_element_type=F32)
    o_ref[...] = (o / denom).astype(BF16)


def _mla_attn(q, k, v, batch, seq):
    t = batch * seq
    tq = MLA_Q_TILE
    nq = seq // tq
    return pl.pallas_call(
        _mla_attn_kernel,
        grid=(batch, MLA_HEADS, nq),
        in_specs=[pl.BlockSpec((tq, MLA_QK_PAD), lambda b, h, i: (b * nq + i, h)),
                  pl.BlockSpec((seq, MLA_QK_PAD), lambda b, h, i: (b, h)),
                  pl.BlockSpec((seq, MLA_VDIM), lambda b, h, i: (b, h))],
        out_specs=pl.BlockSpec((tq, MLA_VDIM), lambda b, h, i: (b * nq + i, h)),
        out_shape=jax.ShapeDtypeStruct((t, MLA_HEADS * MLA_VDIM), BF16),
        compiler_params=pltpu.CompilerParams(dimension_semantics=("arbitrary", "arbitrary", "arbitrary"),
                                             vmem_limit_bytes=VMEM_LIMIT_BYTES),
        name="mla_attn",
    )(q, k, v)


def _mla_block(h, pos, w_in, q_norm_g, kv_norm_g, w_uq, w_ukv, w_out, ln_g, ln_b, batch, seq):
    q, k, v = _mla_in(h, pos, w_in, q_norm_g, kv_norm_g, w_uq, w_ukv)
    o = _mla_attn(q, k, v, batch, seq)
    return _proj_ln(o, w_out, h, ln_g, ln_b)


def kernel(x, positions, ret_w_in, ret_log1m_decay, ret_gn_g, ret_gn_b, ret_w_out, mla_w_in, mla_q_norm,
           mla_kv_norm, mla_w_uq, mla_w_ukv, mla_w_out, peer_w_q, peer_sub_keys, peer_u, peer_v, ln_mix_g,
           ln_mix_b, ln_ffn_g, ln_ffn_b):
    b, s, d = x.shape
    t = b * s
    h = x.reshape(t, d)
    pos = positions.reshape(t)
    for i in range(DEPTH):
        j = i // 2
        if i % 2 == 0:
            h, h8 = _retention_block(h, pos, ret_w_in[j], ret_log1m_decay[j], ret_gn_g[j], ret_gn_b[j],
                                     ret_w_out[j], ln_mix_g[i], ln_mix_b[i], b, s)
        else:
            h, h8 = _mla_block(h, pos, mla_w_in[j], mla_q_norm[j], mla_kv_norm[j], mla_w_uq[j], mla_w_ukv[j],
                               mla_w_out[j], ln_mix_g[i], ln_mix_b[i], b, s)
        h = _peer_block(h, h8, peer_w_q[i], peer_sub_keys[i], peer_u, peer_v, i, ln_ffn_g[i], ln_ffn_b[i])
    return h.reshape(b, s, d)
```

```python
import functools

import jax
import jax.numpy as jnp
from jax import lax
from jax.experimental import pallas as pl
from jax.experimental.pallas import tpu as pltpu

F32 = jnp.float32
BF16 = jnp.bfloat16
I32 = jnp.int32
U32 = jnp.uint32

D_MODEL = 1024
DEPTH = 2
DN_ALPHA = (2.0 * DEPTH) ** 0.25
LN_EPS = 1e-5

PEER_HEADS = 8
PEER_NKEYS = 128
PEER_HALF = 128
PEER_TOPK = 16
PEER_SLOTS = PEER_HEADS * PEER_TOPK
PEER_EXPERTS = PEER_NKEYS * PEER_NKEYS
PEER_HALF_EXPERTS = PEER_EXPERTS // 2

SUBLANES = 8
LANES = 128
VMEM_LIMIT_BYTES = 56 * 1024 * 1024

ROW_TILE = 512
SEL_TOKENS = 256
GATHER_TOKENS = LANES


def _top_rows(st, ids, k, payload=None):
    vals, sel_ids, sel_pay = [], [], []
    big = jnp.int32(2 ** 30)
    for _ in range(k):
        m = jnp.max(st, axis=0, keepdims=True)
        sel = jnp.min(jnp.where(st == m, ids, big), axis=0, keepdims=True)
        hit = ids == sel
        vals.append(m)
        sel_ids.append(sel)
        if payload is not None:
            sel_pay.append(jnp.max(jnp.where(hit, payload, -1), axis=0, keepdims=True))
        st = jnp.where(hit, -jnp.inf, st)
    return vals, sel_ids, sel_pay


def _peer_select_kernel(h_ref, wq_ref, keys_ref, row_ref, up_ref, gate_ref, s_buf, i_buf, e_buf, g_buf):
    tb = h_ref.shape[0]
    q = jnp.dot(h_ref[...].astype(BF16), wq_ref[...], preferred_element_type=F32)
    key_ids = lax.broadcasted_iota(I32, (PEER_NKEYS, tb), 0)
    for head in range(PEER_HEADS):
        for c in range(2):
            col = (head * 2 + c) * PEER_HALF
            qhc = q[:, col:col + PEER_HALF].astype(BF16)
            st = lax.dot_general(keys_ref[head, c], qhc, (((1,), (1,)), ((), ())),
                                 preferred_element_type=F32)
            vals, ids, _ = _top_rows(st, key_ids, PEER_TOPK)
            for a in range(PEER_TOPK):
                s_buf[c, pl.ds(a, 1), :] = vals[a]
                i_buf[c, pl.ds(a, 1), :] = ids[a]
        cand_s, cand_pos, cand_e = [], [], []
        row8 = lax.broadcasted_iota(I32, (SUBLANES, tb), 0)
        for b, a0 in ((0, 0), (0, SUBLANES), (1, 0), (2, 0), (3, 0)):
            a_rows = pl.ds(a0, SUBLANES)
            cand_s.append(s_buf[0, a_rows, :] + s_buf[1, pl.ds(b, 1), :])
            cand_pos.append((row8 + a0) * PEER_TOPK + b)
            cand_e.append(i_buf[0, a_rows, :] * PEER_NKEYS + i_buf[1, pl.ds(b, 1), :])
        for a, b0, b_min, b_max in ((0, 0, 4, 7), (0, SUBLANES, 8, 15), (1, 0, 4, 7), (2, 0, 4, 4)):
            b_rows = pl.ds(b0, SUBLANES)
            bs = row8 + b0
            sums = s_buf[0, pl.ds(a, 1), :] + s_buf[1, b_rows, :]
            cand_s.append(jnp.where((bs >= b_min) & (bs <= b_max), sums, -jnp.inf))
            cand_pos.append(a * PEER_TOPK + bs)
            cand_e.append(i_buf[0, pl.ds(a, 1), :] * PEER_NKEYS + i_buf[1, b_rows, :])
        cand_s = jnp.concatenate(cand_s, axis=0)
        cand_pos = jnp.concatenate(cand_pos, axis=0)
        cand_e = jnp.concatenate(cand_e, axis=0)
        top_s, _, top_e = _top_rows(cand_s, cand_pos, PEER_TOPK, payload=cand_e)
        ex = [jnp.exp(t - top_s[0]) for t in top_s]
        denom = functools.reduce(lambda x, y: x + y, ex)
        inv = 1.0 / denom
        for k in range(PEER_TOPK):
            slot = head * PEER_TOPK + k
            e_buf[pl.ds(slot, 1), :] = top_e[k]
            g_buf[pl.ds(slot, 1), :] = ex[k] * inv
    for s in range(tb // LANES):
        cols = slice(s * LANES, (s + 1) * LANES)
        e = e_buf[:, cols]
        row_ref[s] = (e & (PEER_HALF_EXPERTS - 1)) * SUBLANES
        up_ref[cols, :] = jnp.where(e >= PEER_HALF_EXPERTS, 1.0, 0.0).T
        gate_ref[cols, :] = g_buf[:, cols].T


def _peer_select(h, w_q, sub_keys):
    t = h.shape[0]
    tb = SEL_TOKENS
    wq = w_q.astype(BF16)
    keys = sub_keys.astype(BF16)
    tok = pl.BlockSpec((tb, PEER_SLOTS), lambda i: (i, 0))
    return pl.pallas_call(
        _peer_select_kernel,
        grid=(t // tb,),
        in_specs=[pl.BlockSpec((tb, D_MODEL), lambda i: (i, 0)),
                  pl.BlockSpec(wq.shape, lambda i: (0, 0)),
                  pl.BlockSpec(keys.shape, lambda i: (0, 0, 0, 0))],
        out_specs=[pl.BlockSpec((tb // LANES, PEER_SLOTS, LANES), lambda i: (i, 0, 0)), tok, tok],
        out_shape=[jax.ShapeDtypeStruct((t // LANES, PEER_SLOTS, LANES), I32),
                   jax.ShapeDtypeStruct((t, PEER_SLOTS), F32),
                   jax.ShapeDtypeStruct((t, PEER_SLOTS), F32)],
        scratch_shapes=[pltpu.VMEM((2, PEER_TOPK, tb), F32), pltpu.VMEM((2, PEER_TOPK, tb), I32),
                        pltpu.VMEM((PEER_SLOTS, tb), I32), pltpu.VMEM((PEER_SLOTS, tb), F32)],
        compiler_params=pltpu.CompilerParams(dimension_semantics=("arbitrary",),
                                             vmem_limit_bytes=VMEM_LIMIT_BYTES),
        name="peer_select",
    )(h, wq, keys)


HIGH_HALF = 0xFFFF0000
PACK_ROWS = 512


def _pack_table_kernel(lo_ref, hi_ref, o_ref):
    lo = pltpu.bitcast(lo_ref[...].astype(BF16).astype(F32), U32) >> 16
    hi = pltpu.bitcast(hi_ref[...].astype(BF16).astype(F32), U32) & jnp.uint32(HIGH_HALF)
    packed = lo | hi
    rows = packed.shape[0]
    for c in range(D_MODEL // LANES):
        o_ref[pl.ds(c, rows, stride=SUBLANES), :] = packed[:, c * LANES:(c + 1) * LANES]


def _pack_table(w, layer):
    r = PACK_ROWS
    steps = PEER_HALF_EXPERTS // r
    return pl.pallas_call(
        _pack_table_kernel,
        grid=(steps,),
        in_specs=[pl.BlockSpec((None, r, D_MODEL), lambda i: (layer, i, 0)),
                  pl.BlockSpec((None, r, D_MODEL), lambda i: (layer, i + steps, 0))],
        out_specs=pl.BlockSpec((r * SUBLANES, LANES), lambda i: (i, 0)),
        out_shape=jax.ShapeDtypeStruct((PEER_HALF_EXPERTS * SUBLANES, LANES), U32),
        compiler_params=pltpu.CompilerParams(dimension_semantics=("arbitrary",),
                                             vmem_limit_bytes=VMEM_LIMIT_BYTES),
        name="pack_table",
    )(w, w)


def _packed_rowsum(ws, sub):
    def add(a, b):
        return pltpu.bitcast(pltpu.bitcast(a, BF16) + pltpu.bitcast(b, BF16), U32)

    m4 = sub < 4
    lvl1 = []
    for j in range(4):
        x, y = ws[j], ws[j + 4]
        lvl1.append(add(jnp.where(m4, x, y), pltpu.roll(jnp.where(m4, y, x), 4, 0)))
    m2 = (sub & 2) == 0
    lvl2 = []
    for j in range(2):
        x, y = lvl1[j], lvl1[j + 2]
        lvl2.append(add(jnp.where(m2, x, pltpu.roll(y, 2, 0)), jnp.where(m2, pltpu.roll(x, 6, 0), y)))
    m1 = (sub & 1) == 0
    x, y = lvl2[0], lvl2[1]
    return add(jnp.where(m1, x, pltpu.roll(y, 1, 0)), jnp.where(m1, pltpu.roll(x, 7, 0), y))


def _peer_u_kernel(row_ref, x_ref, up_ref, gate_ref, tbl_ref, c_ref):
    tt = gate_ref.shape[0]
    n_chain = 2
    sub = lax.broadcasted_iota(I32, (SUBLANES, LANES), 0)
    lane = lax.broadcasted_iota(I32, (SUBLANES, LANES), 1)
    high = jnp.uint32(HIGH_HALF)

    def group(g):
        t0 = g * SUBLANES
        xs = []
        for j in range(SUBLANES):
            xrows = pl.ds((t0 + j) * SUBLANES, SUBLANES)
            bits = pltpu.bitcast(x_ref[xrows, :].astype(BF16).astype(F32), U32)
            xs.append(pltpu.bitcast(bits | (bits >> 16), BF16))
        lo = [jnp.zeros((SUBLANES, LANES), F32) for _ in range(n_chain)]
        hi = [jnp.zeros((SUBLANES, LANES), F32) for _ in range(n_chain)]
        for p in range(PEER_SLOTS):
            ws = []
            for j in range(SUBLANES):
                row8 = pl.multiple_of(row_ref[t0 + (p * tt + j)], SUBLANES)
                w = pltpu.bitcast(tbl_ref[pl.ds(row8, SUBLANES), :], BF16)
                ws.append(pltpu.bitcast(w * xs[j], U32))
            r = _packed_rowsum(ws, sub)
            here = lane == p
            c = p % n_chain
            lo[c] = jnp.where(here, jnp.sum(pltpu.bitcast(r << 16, F32), axis=1, keepdims=True), lo[c])
            hi[c] = jnp.where(here, jnp.sum(pltpu.bitcast(r & high, F32), axis=1, keepdims=True), hi[c])
        rows = pl.ds(t0, SUBLANES)
        hdot = jnp.where(up_ref[rows, :] > 0.5, hi[0] + hi[1], lo[0] + lo[1])
        act = 0.5 * hdot * (1.0 + lax.erf(hdot * (2.0 ** -0.5)))
        c_ref[rows, :] = act * gate_ref[rows, :]

    for g in range(tt // SUBLANES):
        group(g)


V_SLOT_RUN = 4


def _peer_v_kernel(row_ref, up_ref, c_ref, tbl_ref, o_ref):
    tt = c_ref.shape[0]
    lane = lax.broadcasted_iota(I32, (SUBLANES, LANES), 1)
    high = jnp.uint32(HIGH_HALF)
    full = (SUBLANES, LANES)

    def group(g):
        t0 = g * SUBLANES
        rows = pl.ds(t0, SUBLANES)
        uppers = up_ref[rows, :]
        coefs = c_ref[rows, :]
        acc_lo = [jnp.zeros(full, F32) for _ in range(SUBLANES)]
        acc_hi = [jnp.zeros(full, F32) for _ in range(SUBLANES)]
        for p0 in range(0, PEER_SLOTS, V_SLOT_RUN):
            prods = [[] for _ in range(SUBLANES)]
            for p in range(p0, p0 + V_SLOT_RUN):
                here = lane == p
                coef = jnp.broadcast_to(jnp.sum(jnp.where(here, coefs, 0.0), axis=1, keepdims=True), full)
                upper = jnp.broadcast_to(jnp.sum(jnp.where(here, uppers, 0.0), axis=1, keepdims=True), full)
                cbits = pltpu.bitcast(coef.astype(BF16).astype(F32), U32)
                cpair = jnp.where(upper > 0.5, cbits, cbits >> 16)
                for j in range(SUBLANES):
                    row8 = pl.multiple_of(row_ref[t0 + (p * tt + j)], SUBLANES)
                    w = pltpu.bitcast(tbl_ref[pl.ds(row8, SUBLANES), :], BF16)
                    cj = pltpu.bitcast(jnp.broadcast_to(cpair[j:j + 1, :], full), BF16)
                    prods[j].append(w * cj)
            for j in range(SUBLANES):
                pr = prods[j]
                while len(pr) > 1:
                    pr = [pr[i] + pr[i + 1] for i in range(0, len(pr), 2)]
                s = pltpu.bitcast(pr[0], U32)
                acc_lo[j] = acc_lo[j] + pltpu.bitcast(s << 16, F32)
                acc_hi[j] = acc_hi[j] + pltpu.bitcast(s & high, F32)
        for j in range(SUBLANES):
            o_ref[pl.ds((t0 + j) * SUBLANES, SUBLANES), :] = acc_lo[j] + acc_hi[j]

    for g in range(tt // SUBLANES):
        group(g)


def _peer_retrieve(x8, row8, upper, gate, u_tbl, v_tbl):
    t = gate.shape[0]
    tt = GATHER_TOKENS
    smem_spec = pl.BlockSpec((tt * PEER_SLOTS,), lambda i: (i,), memory_space=pltpu.SMEM,
                             pipeline_mode=pl.Buffered(1))
    slots = pl.BlockSpec((tt, PEER_SLOTS), lambda i: (i, 0))
    rows8 = pl.BlockSpec((tt * SUBLANES, LANES), lambda i: (i, 0))
    table_spec = pl.BlockSpec(memory_space=pltpu.VMEM)
    params = pltpu.CompilerParams(dimension_semantics=("arbitrary",), vmem_limit_bytes=VMEM_LIMIT_BYTES)
    row8 = row8.reshape(-1)
    coef = pl.pallas_call(
        _peer_u_kernel,
        grid=(t // tt,),
        in_specs=[smem_spec, rows8, slots, slots, table_spec],
        out_specs=slots,
        out_shape=jax.ShapeDtypeStruct((t, PEER_SLOTS), F32),
        compiler_params=params,
        name="peer_u",
    )(row8, x8, upper, gate, u_tbl)
    return pl.pallas_call(
        _peer_v_kernel,
        grid=(t // tt,),
        in_specs=[smem_spec, slots, slots, table_spec],
        out_specs=rows8,
        out_shape=jax.ShapeDtypeStruct((t * SUBLANES, LANES), F32),
        compiler_params=params,
        name="peer_v",
    )(row8, upper, coef, v_tbl)


def _layer_norm(y, g, b):
    mu = jnp.mean(y, axis=-1, keepdims=True)
    d = y - mu
    var = jnp.mean(d * d, axis=-1, keepdims=True)
    return d * lax.rsqrt(var + LN_EPS) * g + b


def _store_tiles(o8_ref, out):
    tm = out.shape[0]
    for c in range(D_MODEL // LANES):
        o8_ref[pl.ds(c, tm, stride=SUBLANES), :] = out[:, c * LANES:(c + 1) * LANES]


def _add_ln_kernel(h_ref, f8_ref, g_ref, b_ref, o_ref):
    tm = h_ref.shape[0]
    f = jnp.concatenate([f8_ref[pl.ds(c, tm, stride=SUBLANES), :] for c in range(D_MODEL // LANES)], axis=1)
    o_ref[...] = _layer_norm(DN_ALPHA * h_ref[...] + f, g_ref[...], b_ref[...])


def _add_ln(h, f8, g, b):
    t, d = h.shape
    tb = ROW_TILE
    row = pl.BlockSpec((tb, d), lambda i: (i, 0))
    vec = pl.BlockSpec((1, d), lambda i: (0, 0))
    return pl.pallas_call(
        _add_ln_kernel,
        grid=(t // tb,),
        in_specs=[row, pl.BlockSpec((tb * SUBLANES, LANES), lambda i: (i, 0)), vec, vec],
        out_specs=row,
        out_shape=jax.ShapeDtypeStruct((t, d), F32),
        compiler_params=pltpu.CompilerParams(dimension_semantics=("arbitrary",)),
        name="add_ln",
    )(h, f8, g.reshape(1, d), b.reshape(1, d))


def _peer_block(h, h8, w_q, sub_keys, u_all, v_all, layer, ln_g, ln_b):
    row8, upper, gate = _peer_select(h, w_q, sub_keys)
    f8 = _peer_retrieve(h8, row8, upper, gate, _pack_table(u_all, layer), _pack_table(v_all, layer))
    return _add_ln(h, f8, ln_g, ln_b)


def _proj_ln_kernel(a_ref, w_ref, h_ref, g_ref, b_ref, o_ref, o8_ref):
    y = DN_ALPHA * h_ref[...] + jnp.dot(a_ref[...], w_ref[...], preferred_element_type=F32)
    out = _layer_norm(y, g_ref[...], b_ref[...])
    o_ref[...] = out
    _store_tiles(o8_ref, out)


def _proj_ln(a, w, h, g, b):
    t, k = a.shape
    d = w.shape[1]
    tm = ROW_TILE
    vec = pl.BlockSpec((1, d), lambda i: (0, 0))
    return pl.pallas_call(
        _proj_ln_kernel,
        grid=(t // tm,),
        in_specs=[pl.BlockSpec((tm, k), lambda i: (i, 0)), pl.BlockSpec((k, d), lambda i: (0, 0)),
                  pl.BlockSpec((tm, d), lambda i: (i, 0)), vec, vec],
        out_specs=[pl.BlockSpec((tm, d), lambda i: (i, 0)),
                   pl.BlockSpec((tm * SUBLANES, LANES), lambda i: (i, 0))],
        out_shape=[jax.ShapeDtypeStruct((t, d), F32), jax.ShapeDtypeStruct((t * SUBLANES, LANES), F32)],
        compiler_params=pltpu.CompilerParams(dimension_semantics=("arbitrary",),
                                             vmem_limit_bytes=VMEM_LIMIT_BYTES),
        name="proj_ln",
    )(a, w.astype(BF16), h, g.reshape(1, d), b.reshape(1, d))


RET_HEADS = 4
RET_DK = 256
RET_DV = 512
RET_HK = RET_HEADS * RET_DK
RET_HV = RET_HEADS * RET_DV
RET_CHUNK = 256
ROPE_BASE = 10000.0
RET_COL_TILE = 256


def _ret_in_kernel(x_ref, pos_ref, freq_ref, w_ref, qkv_ref, gate_ref):
    tn = RET_COL_TILE
    n_qk = 2 * RET_HK // tn
    n_qkv = n_qk + RET_HV // tn
    half = RET_DK // 2
    x = x_ref[...].astype(BF16)
    ang = pos_ref[...].astype(F32) * freq_ref[...]
    c, s = jnp.cos(ang), jnp.sin(ang)
    for j in range(w_ref.shape[1] // tn):
        y = jnp.dot(x, w_ref[:, j * tn:(j + 1) * tn], preferred_element_type=F32)
        if j < n_qk:
            x1, x2 = y[:, :half], y[:, half:]
            scale = 1.0 if j < n_qk // 2 else RET_DK ** -0.5
            qkv_ref[:, j * tn:j * tn + half] = ((x1 * c - x2 * s) * scale).astype(BF16)
            qkv_ref[:, j * tn + half:(j + 1) * tn] = ((x1 * s + x2 * c) * scale).astype(BF16)
        elif j < n_qkv:
            qkv_ref[:, j * tn:(j + 1) * tn] = y.astype(BF16)
        else:
            gate_ref[:, (j - n_qkv) * tn:(j - n_qkv + 1) * tn] = y


def _ret_in(x, pos, w_in):
    t, d = x.shape
    tm = ROW_TILE
    n_all = w_in.shape[1]
    n_qkv = 2 * RET_HK + RET_HV
    half = RET_DK // 2
    freq = (ROPE_BASE ** (-jnp.arange(0, RET_DK, 2, dtype=F32) / RET_DK)).reshape(1, half)
    return pl.pallas_call(
        _ret_in_kernel,
        grid=(t // tm,),
        in_specs=[pl.BlockSpec((tm, d), lambda i: (i, 0)),
                  pl.BlockSpec((tm, 1), lambda i: (i, 0)),
                  pl.BlockSpec((1, half), lambda i: (0, 0)),
                  pl.BlockSpec((d, n_all), lambda i: (0, 0))],
        out_specs=[pl.BlockSpec((tm, n_qkv), lambda i: (i, 0)),
                   pl.BlockSpec((tm, n_all - n_qkv), lambda i: (i, 0))],
        out_shape=[jax.ShapeDtypeStruct((t, n_qkv), BF16),
                   jax.ShapeDtypeStruct((t, n_all - n_qkv), F32)],
        compiler_params=pltpu.CompilerParams(dimension_semantics=("arbitrary",),
                                             vmem_limit_bytes=VMEM_LIMIT_BYTES),
        name="ret_in",
    )(x, pos.reshape(t, 1), freq, w_in.astype(BF16))


def _ret_scan_kernel(q_ref, k_ref, v_ref, gate_ref, dec_ref, xi_ref, zeta_ref, gc_ref, gng_ref, gnb_ref,
                     o_ref, state_ref, y_ref):
    c = RET_CHUNK
    n_chunks = q_ref.shape[0] // c

    def chunk_out(i, d):
        rows = pl.ds(pl.multiple_of(i * c, c), c)
        qi, ki, vi = q_ref[rows, :], k_ref[rows, :], v_ref[rows, :]
        sc = lax.dot_general(qi, ki, (((1,), (1,)), ((), ())), preferred_element_type=F32)
        inner = jnp.dot((sc * dec_ref[d]).astype(BF16), vi, preferred_element_type=F32)
        cross = jnp.dot(qi, state_ref[...].astype(BF16), preferred_element_type=F32) * xi_ref[d]
        kz = (ki.astype(F32) * zeta_ref[d]).astype(BF16)
        upd = lax.dot_general(kz, vi, (((0,), (0,)), ((), ())), preferred_element_type=F32)
        state_ref[...] = gc_ref[d] * state_ref[...] + upd
        return rows, inner + cross

    state_ref[...] = jnp.zeros_like(state_ref)

    def fwd(i, carry):
        rows, y = chunk_out(i, 0)
        y_ref[rows, :] = y
        return carry

    lax.fori_loop(0, n_chunks, fwd, 0)
    state_ref[...] = jnp.zeros_like(state_ref)

    def bwd(n, carry):
        rows, y = chunk_out(n_chunks - 1 - n, 1)
        y = y + y_ref[rows, :]
        mu = jnp.mean(y, axis=-1, keepdims=True)
        dlt = y - mu
        var = jnp.mean(dlt * dlt, axis=-1, keepdims=True)
        yn = dlt * lax.rsqrt(var + LN_EPS) * gng_ref[...] + gnb_ref[...]
        g = gate_ref[rows, :]
        o_ref[rows, :] = (g * (1.0 / (1.0 + jnp.exp(-g))) * yn).astype(BF16)
        return carry

    lax.fori_loop(0, n_chunks, bwd, 0)


def _ret_scan(qkv, gate, log1m_decay, gn_g, gn_b, batch, seq):
    c = RET_CHUNK
    log_gamma = jnp.log1p(-jnp.exp(log1m_decay.astype(F32)))
    idx = jnp.arange(c, dtype=F32)
    diff = idx[:, None] - idx[None, :]
    lg = log_gamma[:, :, None, None]
    dec_f = jnp.where(diff >= 0, jnp.exp(jnp.maximum(diff, 0.0) * lg[0]), 0.0)
    dec_b = jnp.where(diff < 0, jnp.exp(jnp.maximum(-diff, 0.0) * lg[1]), 0.0)
    dec = jnp.stack([dec_f, dec_b], axis=1)
    lgc = log_gamma[:, :, None]
    xi = jnp.stack([jnp.exp((idx + 1.0) * lgc[0]), jnp.exp((c - idx) * lgc[1])], axis=1)[..., None]
    zeta = jnp.stack([jnp.exp((c - 1.0 - idx) * lgc[0]), jnp.exp(idx * lgc[1])], axis=1)[..., None]
    gc = jnp.exp(c * log_gamma).T.reshape(RET_HEADS, 2, 1, 1)
    t = batch * seq
    kq = RET_HK // RET_DK
    kv = 2 * RET_HK // RET_DV
    head4 = lambda b, h: (h, 0, 0, 0)
    return pl.pallas_call(
        _ret_scan_kernel,
        grid=(batch, RET_HEADS),
        in_specs=[pl.BlockSpec((seq, RET_DK), lambda b, h: (b, h)),
                  pl.BlockSpec((seq, RET_DK), lambda b, h: (b, kq + h)),
                  pl.BlockSpec((seq, RET_DV), lambda b, h: (b, kv + h)),
                  pl.BlockSpec((seq, RET_DV), lambda b, h: (b, h)),
                  pl.BlockSpec((None, 2, c, c), head4),
                  pl.BlockSpec((None, 2, c, 1), head4),
                  pl.BlockSpec((None, 2, c, 1), head4),
                  pl.BlockSpec((None, 2, 1, 1), head4),
                  pl.BlockSpec((1, RET_DV), lambda b, h: (0, h)),
                  pl.BlockSpec((1, RET_DV), lambda b, h: (0, h))],
        out_specs=pl.BlockSpec((seq, RET_DV), lambda b, h: (b, h)),
        out_shape=jax.ShapeDtypeStruct((t, RET_HV), BF16),
        scratch_shapes=[pltpu.VMEM((RET_DK, RET_DV), F32), pltpu.VMEM((seq, RET_DV), F32)],
        compiler_params=pltpu.CompilerParams(dimension_semantics=("arbitrary", "arbitrary"),
                                             vmem_limit_bytes=VMEM_LIMIT_BYTES),
        name="ret_scan",
    )(qkv, qkv, qkv, gate, dec, xi, zeta, gc, gn_g.reshape(1, RET_HV), gn_b.reshape(1, RET_HV))


def _retention_block(h, pos, w_in, log1m_decay, gn_g, gn_b, w_out, ln_g, ln_b, batch, seq):
    qkv, gate = _ret_in(h, pos, w_in)
    gated = _ret_scan(qkv, gate, log1m_decay, gn_g, gn_b, batch, seq)
    return _proj_ln(gated, w_out, h, ln_g, ln_b)


MLA_HEADS = 8
MLA_NOPE = 128
MLA_ROPE = 64
MLA_VDIM = 128
MLA_Q_RANK = 384
MLA_KV_RANK = 256
MLA_QK_PAD = 256
MLA_Q_TILE = 256


def _rms(x, g):
    ms = jnp.mean(x * x, axis=-1, keepdims=True)
    return x * lax.rsqrt(ms + LN_EPS) * g


def _mla_in_kernel(x_ref, pos_ref, freq_ref, win_ref, qg_ref, kvg_ref, wuq_ref, wukv_ref, q_ref, k_ref, v_ref):
    c = jnp.dot(x_ref[...].astype(BF16), win_ref[...], preferred_element_type=F32)
    lane = lax.broadcasted_iota(I32, (1, LANES), 1)
    half = MLA_ROPE // 2
    ang = pos_ref[...].astype(F32) * freq_ref[...]
    cosv = jnp.where(lane < MLA_ROPE, jnp.cos(ang), 0.0)
    sinv = jnp.sin(ang)
    sinv = jnp.where(lane < half, -sinv, jnp.where(lane < MLA_ROPE, sinv, 0.0))

    def rope(y):
        swapped = pltpu.roll(y, half, 1) + pltpu.roll(y, LANES - half, 1)
        return y * cosv + swapped * sinv

    cq = _rms(c[:, :MLA_Q_RANK], qg_ref[...]).astype(BF16)
    ckv = _rms(c[:, MLA_Q_RANK:MLA_Q_RANK + MLA_KV_RANK], kvg_ref[...]).astype(BF16)
    k_rope = rope(c[:, MLA_Q_RANK + MLA_KV_RANK:]).astype(BF16)
    q = jnp.dot(cq, wuq_ref[...], preferred_element_type=F32)
    kv = jnp.dot(ckv, wukv_ref[...], preferred_element_type=F32)
    for hd in range(MLA_HEADS):
        o = hd * MLA_QK_PAD
        q_ref[:, o:o + MLA_NOPE] = q[:, o:o + MLA_NOPE].astype(BF16)
        q_ref[:, o + MLA_NOPE:o + MLA_QK_PAD] = rope(q[:, o + MLA_NOPE:o + MLA_QK_PAD]).astype(BF16)
        k_ref[:, o:o + MLA_NOPE] = kv[:, o:o + MLA_NOPE].astype(BF16)
        k_ref[:, o + MLA_NOPE:o + MLA_QK_PAD] = k_rope
        v_ref[:, hd * MLA_VDIM:(hd + 1) * MLA_VDIM] = kv[:, o + MLA_NOPE:o + MLA_QK_PAD].astype(BF16)


def _mla_in(x, pos, w_in, q_norm_g, kv_norm_g, w_uq, w_ukv):
    t, d = x.shape
    tm = ROW_TILE
    pad = MLA_QK_PAD - MLA_NOPE - MLA_ROPE
    win = jnp.pad(w_in, ((0, 0), (0, pad))).astype(BF16)
    wuq = jnp.pad(w_uq.reshape(MLA_Q_RANK, MLA_HEADS, MLA_NOPE + MLA_ROPE), ((0, 0), (0, 0), (0, pad)))
    wuq = wuq.reshape(MLA_Q_RANK, MLA_HEADS * MLA_QK_PAD).astype(BF16)
    wukv = w_ukv.astype(BF16)
    f = ROPE_BASE ** (-jnp.arange(0, MLA_ROPE, 2, dtype=F32) / MLA_ROPE)
    freq = jnp.concatenate([f, f, jnp.zeros((LANES - MLA_ROPE,), F32)]).reshape(1, LANES)
    full = lambda a: pl.BlockSpec(a.shape, lambda i: (0,) * a.ndim)
    qg = q_norm_g.reshape(1, -1)
    kvg = kv_norm_g.reshape(1, -1)
    wide = MLA_HEADS * MLA_QK_PAD
    return pl.pallas_call(
        _mla_in_kernel,
        grid=(t // tm,),
        in_specs=[pl.BlockSpec((tm, d), lambda i: (i, 0)), pl.BlockSpec((tm, 1), lambda i: (i, 0)),
                  full(freq), full(win), full(qg), full(kvg), full(wuq), full(wukv)],
        out_specs=[pl.BlockSpec((tm, wide), lambda i: (i, 0)), pl.BlockSpec((tm, wide), lambda i: (i, 0)),
                   pl.BlockSpec((tm, MLA_HEADS * MLA_VDIM), lambda i: (i, 0))],
        out_shape=[jax.ShapeDtypeStruct((t, wide), BF16), jax.ShapeDtypeStruct((t, wide), BF16),
                   jax.ShapeDtypeStruct((t, MLA_HEADS * MLA_VDIM), BF16)],
        compiler_params=pltpu.CompilerParams(dimension_semantics=("arbitrary",),
                                             vmem_limit_bytes=VMEM_LIMIT_BYTES),
        name="mla_in",
    )(x, pos.reshape(t, 1), freq, win, qg, kvg, wuq, wukv)


def _mla_attn_kernel(q_ref, k_ref, v_ref, o_ref):
    s = lax.dot_general(q_ref[...], k_ref[...], (((1,), (1,)), ((), ())), preferred_element_type=F32)
    s = s * ((MLA_NOPE + MLA_ROPE) ** -0.5)
    p = jnp.exp(s - jnp.max(s, axis=-1, keepdims=True))
    denom = jnp.sum(p, axis=-1, keepdims=True)
    o = jnp.dot(p.astype(BF16), v_ref[...], preferred_element_type=F32)
    o_ref[...] = (o / denom).astype(BF16)


def _mla_attn(q, k, v, batch, seq):
    t = batch * seq
    tq = MLA_Q_TILE
    nq = seq // tq
    return pl.pallas_call(
        _mla_attn_kernel,
        grid=(batch, MLA_HEADS, nq),
        in_specs=[pl.BlockSpec((tq, MLA_QK_PAD), lambda b, h, i: (b * nq + i, h)),
                  pl.BlockSpec((seq, MLA_QK_PAD), lambda b, h, i: (b, h)),
                  pl.BlockSpec((seq, MLA_VDIM), lambda b, h, i: (b, h))],
        out_specs=pl.BlockSpec((tq, MLA_VDIM), lambda b, h, i: (b * nq + i, h)),
        out_shape=jax.ShapeDtypeStruct((t, MLA_HEADS * MLA_VDIM), BF16),
        compiler_params=pltpu.CompilerParams(dimension_semantics=("arbitrary", "arbitrary", "arbitrary"),
                                             vmem_limit_bytes=VMEM_LIMIT_BYTES),
        name="mla_attn",
    )(q, k, v)


def _mla_block(h, pos, w_in, q_norm_g, kv_norm_g, w_uq, w_ukv, w_out, ln_g, ln_b, batch, seq):
    q, k, v = _mla_in(h, pos, w_in, q_norm_g, kv_norm_g, w_uq, w_ukv)
    o = _mla_attn(q, k, v, batch, seq)
    return _proj_ln(o, w_out, h, ln_g, ln_b)


def kernel(x, positions, ret_w_in, ret_log1m_decay, ret_gn_g, ret_gn_b, ret_w_out, mla_w_in, mla_q_norm,
           mla_kv_norm, mla_w_uq, mla_w_ukv, mla_w_out, peer_w_q, peer_sub_keys, peer_u, peer_v, ln_mix_g,
           ln_mix_b, ln_ffn_g, ln_ffn_b):
    b, s, d = x.shape
    t = b * s
    h = x.reshape(t, d)
    pos = positions.reshape(t)
    for i in range(DEPTH):
        j = i // 2
        if i % 2 == 0:
            h, h8 = _retention_block(h, pos, ret_w_in[j], ret_log1m_decay[j], ret_gn_g[j], ret_gn_b[j],
                                     ret_w_out[j], ln_mix_g[i], ln_mix_b[i], b, s)
        else:
            h, h8 = _mla_block(h, pos, mla_w_in[j], mla_q_norm[j], mla_kv_norm[j], mla_w_uq[j], mla_w_ukv[j],
                               mla_w_out[j], ln_mix_g[i], ln_mix_b[i], b, s)
        h = _peer_block(h, h8, peer_w_q[i], peer_sub_keys[i], peer_u, peer_v, i, ln_ffn_g[i], ln_ffn_b[i])
    return h.reshape(b, s, d)
```

```python
import functools

import jax
import jax.numpy as jnp
from jax import lax
from jax.experimental import pallas as pl
from jax.experimental.pallas import tpu as pltpu

F32 = jnp.float32
BF16 = jnp.bfloat16
I32 = jnp.int32
U32 = jnp.uint32

D_MODEL = 1024
DEPTH = 2
DN_ALPHA = (2.0 * DEPTH) ** 0.25
LN_EPS = 1e-5

PEER_HEADS = 8
PEER_NKEYS = 128
PEER_HALF = 128
PEER_TOPK = 16
PEER_SLOTS = PEER_HEADS * PEER_TOPK
PEER_EXPERTS = PEER_NKEYS * PEER_NKEYS
PEER_HALF_EXPERTS = PEER_EXPERTS // 2

SUBLANES = 8
LANES = 128
VMEM_LIMIT_BYTES = 56 * 1024 * 1024

ROW_TILE = 512
SEL_TOKENS = 256
GATHER_TOKENS = LANES


def _top_rows(st, ids, k, payload=None):
    vals, sel_ids, sel_pay = [], [], []
    big = jnp.int32(2 ** 30)
    for _ in range(k):
        m = jnp.max(st, axis=0, keepdims=True)
        sel = jnp.min(jnp.where(st == m, ids, big), axis=0, keepdims=True)
        hit = ids == sel
        vals.append(m)
        sel_ids.append(sel)
        if payload is not None:
            sel_pay.append(jnp.max(jnp.where(hit, payload, -1), axis=0, keepdims=True))
        st = jnp.where(hit, -jnp.inf, st)
    return vals, sel_ids, sel_pay


def _peer_select_kernel(h_ref, wq_ref, keys_ref, row_ref, up_ref, gate_ref, s_buf, i_buf, e_buf, g_buf):
    tb = h_ref.shape[0]
    q = jnp.dot(h_ref[...].astype(BF16), wq_ref[...], preferred_element_type=F32)
    key_ids = lax.broadcasted_iota(I32, (PEER_NKEYS, tb), 0)
    for head in range(PEER_HEADS):
        for c in range(2):
            col = (head * 2 + c) * PEER_HALF
            qhc = q[:, col:col + PEER_HALF].astype(BF16)
            st = lax.dot_general(keys_ref[head, c], qhc, (((1,), (1,)), ((), ())),
                                 preferred_element_type=F32)
            vals, ids, _ = _top_rows(st, key_ids, PEER_TOPK)
            for a in range(PEER_TOPK):
                s_buf[c, pl.ds(a, 1), :] = vals[a]
                i_buf[c, pl.ds(a, 1), :] = ids[a]
        cand_s, cand_pos, cand_e = [], [], []
        row8 = lax.broadcasted_iota(I32, (SUBLANES, tb), 0)
        for b, a0 in ((0, 0), (0, SUBLANES), (1, 0), (2, 0), (3, 0)):
            a_rows = pl.ds(a0, SUBLANES)
            cand_s.append(s_buf[0, a_rows, :] + s_buf[1, pl.ds(b, 1), :])
            cand_pos.append((row8 + a0) * PEER_TOPK + b)
            cand_e.append(i_buf[0, a_rows, :] * PEER_NKEYS + i_buf[1, pl.ds(b, 1), :])
        for a, b0, b_min, b_max in ((0, 0, 4, 7), (0, SUBLANES, 8, 15), (1, 0, 4, 7), (2, 0, 4, 4)):
            b_rows = pl.ds(b0, SUBLANES)
            bs = row8 + b0
            sums = s_buf[0, pl.ds(a, 1), :] + s_buf[1, b_rows, :]
            cand_s.append(jnp.where((bs >= b_min) & (bs <= b_max), sums, -jnp.inf))
            cand_pos.append(a * PEER_TOPK + bs)
            cand_e.append(i_buf[0, pl.ds(a, 1), :] * PEER_NKEYS + i_buf[1, b_rows, :])
        cand_s = jnp.concatenate(cand_s, axis=0)
        cand_pos = jnp.concatenate(cand_pos, axis=0)
        cand_e = jnp.concatenate(cand_e, axis=0)
        top_s, _, top_e = _top_rows(cand_s, cand_pos, PEER_TOPK, payload=cand_e)
        ex = [jnp.exp(t - top_s[0]) for t in top_s]
        denom = functools.reduce(lambda x, y: x + y, ex)
        inv = 1.0 / denom
        for k in range(PEER_TOPK):
            slot = head * PEER_TOPK + k
            e_buf[pl.ds(slot, 1), :] = top_e[k]
            g_buf[pl.ds(slot, 1), :] = ex[k] * inv
    for s in range(tb // LANES):
        cols = slice(s * LANES, (s + 1) * LANES)
        e = e_buf[:, cols]
        row_ref[s] = (e & (PEER_HALF_EXPERTS - 1)) * SUBLANES
        up_ref[cols, :] = jnp.where(e >= PEER_HALF_EXPERTS, 1.0, 0.0).T
        gate_ref[cols, :] = g_buf[:, cols].T


def _peer_select(h, w_q, sub_keys):
    t = h.shape[0]
    tb = SEL_TOKENS
    wq = w_q.astype(BF16)
    keys = sub_keys.astype(BF16)
    tok = pl.BlockSpec((tb, PEER_SLOTS), lambda i: (i, 0))
    return pl.pallas_call(
        _peer_select_kernel,
        grid=(t // tb,),
        in_specs=[pl.BlockSpec((tb, D_MODEL), lambda i: (i, 0)),
                  pl.BlockSpec(wq.shape, lambda i: (0, 0)),
                  pl.BlockSpec(keys.shape, lambda i: (0, 0, 0, 0))],
        out_specs=[pl.BlockSpec((tb // LANES, PEER_SLOTS, LANES), lambda i: (i, 0, 0)), tok, tok],
        out_shape=[jax.ShapeDtypeStruct((t // LANES, PEER_SLOTS, LANES), I32),
                   jax.ShapeDtypeStruct((t, PEER_SLOTS), F32),
                   jax.ShapeDtypeStruct((t, PEER_SLOTS), F32)],
        scratch_shapes=[pltpu.VMEM((2, PEER_TOPK, tb), F32), pltpu.VMEM((2, PEER_TOPK, tb), I32),
                        pltpu.VMEM((PEER_SLOTS, tb), I32), pltpu.VMEM((PEER_SLOTS, tb), F32)],
        compiler_params=pltpu.CompilerParams(dimension_semantics=("arbitrary",),
                                             vmem_limit_bytes=VMEM_LIMIT_BYTES),
        name="peer_select",
    )(h, wq, keys)


HIGH_HALF = 0xFFFF0000
PACK_ROWS = 512


def _pack_table_kernel(lo_ref, hi_ref, o_ref):
    lo = pltpu.bitcast(lo_ref[...].astype(BF16).astype(F32), U32) >> 16
    hi = pltpu.bitcast(hi_ref[...].astype(BF16).astype(F32), U32) & jnp.uint32(HIGH_HALF)
    packed = lo | hi
    rows = packed.shape[0]
    for c in range(D_MODEL // LANES):
        o_ref[pl.ds(c, rows, stride=SUBLANES), :] = packed[:, c * LANES:(c + 1) * LANES]


def _pack_table(w, layer):
    r = PACK_ROWS
    steps = PEER_HALF_EXPERTS // r
    return pl.pallas_call(
        _pack_table_kernel,
        grid=(steps,),
        in_specs=[pl.BlockSpec((None, r, D_MODEL), lambda i: (layer, i, 0)),
                  pl.BlockSpec((None, r, D_MODEL), lambda i: (layer, i + steps, 0))],
        out_specs=pl.BlockSpec((r * SUBLANES, LANES), lambda i: (i, 0)),
        out_shape=jax.ShapeDtypeStruct((PEER_HALF_EXPERTS * SUBLANES, LANES), U32),
        compiler_params=pltpu.CompilerParams(dimension_semantics=("arbitrary",),
                                             vmem_limit_bytes=VMEM_LIMIT_BYTES),
        name="pack_table",
    )(w, w)


def _packed_rowsum(ws, sub):
    def add(a, b):
        return pltpu.bitcast(pltpu.bitcast(a, BF16) + pltpu.bitcast(b, BF16), U32)

    m4 = sub < 4
    lvl1 = []
    for j in range(4):
        x, y = ws[j], ws[j + 4]
        lvl1.append(add(jnp.where(m4, x, y), pltpu.roll(jnp.where(m4, y, x), 4, 0)))
    m2 = (sub & 2) == 0
    lvl2 = []
    for j in range(2):
        x, y = lvl1[j], lvl1[j + 2]
        lvl2.append(add(jnp.where(m2, x, pltpu.roll(y, 2, 0)), jnp.where(m2, pltpu.roll(x, 6, 0), y)))
    m1 = (sub & 1) == 0
    x, y = lvl2[0], lvl2[1]
    return add(jnp.where(m1, x, pltpu.roll(y, 1, 0)), jnp.where(m1, pltpu.roll(x, 7, 0), y))


def _peer_u_kernel(row_ref, x_ref, up_ref, gate_ref, tbl_ref, c_ref):
    tt = gate_ref.shape[0]
    n_chain = 2
    sub = lax.broadcasted_iota(I32, (SUBLANES, LANES), 0)
    lane = lax.broadcasted_iota(I32, (SUBLANES, LANES), 1)
    high = jnp.uint32(HIGH_HALF)

    def group(g):
        t0 = g * SUBLANES
        xs = []
        for j in range(SUBLANES):
            xrows = pl.ds((t0 + j) * SUBLANES, SUBLANES)
            bits = pltpu.bitcast(x_ref[xrows, :].astype(BF16).astype(F32), U32)
            xs.append(pltpu.bitcast(bits | (bits >> 16), BF16))
        lo = [jnp.zeros((SUBLANES, LANES), F32) for _ in range(n_chain)]
        hi = [jnp.zeros((SUBLANES, LANES), F32) for _ in range(n_chain)]
        for p in range(PEER_SLOTS):
            ws = []
            for j in range(SUBLANES):
                row8 = pl.multiple_of(row_ref[t0 + (p * tt + j)], SUBLANES)
                w = pltpu.bitcast(tbl_ref[pl.ds(row8, SUBLANES), :], BF16)
                ws.append(pltpu.bitcast(w * xs[j], U32))
            r = _packed_rowsum(ws, sub)
            here = lane == p
            c = p % n_chain
            lo[c] = jnp.where(here, jnp.sum(pltpu.bitcast(r << 16, F32), axis=1, keepdims=True), lo[c])
            hi[c] = jnp.where(here, jnp.sum(pltpu.bitcast(r & high, F32), axis=1, keepdims=True), hi[c])
        rows = pl.ds(t0, SUBLANES)
        hdot = jnp.where(up_ref[rows, :] > 0.5, hi[0] + hi[1], lo[0] + lo[1])
        act = 0.5 * hdot * (1.0 + lax.erf(hdot * (2.0 ** -0.5)))
        c_ref[rows, :] = act * gate_ref[rows, :]

    for g in range(tt // SUBLANES):
        group(g)


V_SLOT_RUN = 8


def _peer_v_kernel(row_ref, up_ref, c_ref, tbl_ref, o_ref):
    tt = c_ref.shape[0]
    lane = lax.broadcasted_iota(I32, (SUBLANES, LANES), 1)
    high = jnp.uint32(HIGH_HALF)
    full = (SUBLANES, LANES)

    def group(g):
        t0 = g * SUBLANES
        rows = pl.ds(t0, SUBLANES)
        uppers = up_ref[rows, :]
        coefs = c_ref[rows, :]
        acc_lo = [jnp.zeros(full, F32) for _ in range(SUBLANES)]
        acc_hi = [jnp.zeros(full, F32) for _ in range(SUBLANES)]
        for p0 in range(0, PEER_SLOTS, V_SLOT_RUN):
            prods = [[] for _ in range(SUBLANES)]
            for p in range(p0, p0 + V_SLOT_RUN):
                here = lane == p
                coef = jnp.broadcast_to(jnp.sum(jnp.where(here, coefs, 0.0), axis=1, keepdims=True), full)
                upper = jnp.broadcast_to(jnp.sum(jnp.where(here, uppers, 0.0), axis=1, keepdims=True), full)
                cbits = pltpu.bitcast(coef.astype(BF16).astype(F32), U32)
                cpair = jnp.where(upper > 0.5, cbits, cbits >> 16)
                for j in range(SUBLANES):
                    row8 = pl.multiple_of(row_ref[t0 + (p * tt + j)], SUBLANES)
                    w = pltpu.bitcast(tbl_ref[pl.ds(row8, SUBLANES), :], BF16)
                    cj = pltpu.bitcast(jnp.broadcast_to(cpair[j:j + 1, :], full), BF16)
                    prods[j].append(w * cj)
            for j in range(SUBLANES):
                pr = prods[j]
                while len(pr) > 1:
                    pr = [pr[i] + pr[i + 1] for i in range(0, len(pr), 2)]
                s = pltpu.bitcast(pr[0], U32)
                acc_lo[j] = acc_lo[j] + pltpu.bitcast(s << 16, F32)
                acc_hi[j] = acc_hi[j] + pltpu.bitcast(s & high, F32)
        for j in range(SUBLANES):
            o_ref[pl.ds((t0 + j) * SUBLANES, SUBLANES), :] = acc_lo[j] + acc_hi[j]

    for g in range(tt // SUBLANES):
        group(g)


def _peer_retrieve(x8, row8, upper, gate, u_tbl, v_tbl):
    t = gate.shape[0]
    tt = GATHER_TOKENS
    smem_spec = pl.BlockSpec((tt * PEER_SLOTS,), lambda i: (i,), memory_space=pltpu.SMEM,
                             pipeline_mode=pl.Buffered(1))
    slots = pl.BlockSpec((tt, PEER_SLOTS), lambda i: (i, 0))
    rows8 = pl.BlockSpec((tt * SUBLANES, LANES), lambda i: (i, 0))
    table_spec = pl.BlockSpec(memory_space=pltpu.VMEM)
    params = pltpu.CompilerParams(dimension_semantics=("arbitrary",), vmem_limit_bytes=VMEM_LIMIT_BYTES)
    row8 = row8.reshape(-1)
    coef = pl.pallas_call(
        _peer_u_kernel,
        grid=(t // tt,),
        in_specs=[smem_spec, rows8, slots, slots, table_spec],
        out_specs=slots,
        out_shape=jax.ShapeDtypeStruct((t, PEER_SLOTS), F32),
        compiler_params=params,
        name="peer_u",
    )(row8, x8, upper, gate, u_tbl)
    return pl.pallas_call(
        _peer_v_kernel,
        grid=(t // tt,),
        in_specs=[smem_spec, slots, slots, table_spec],
        out_specs=rows8,
        out_shape=jax.ShapeDtypeStruct((t * SUBLANES, LANES), F32),
        compiler_params=params,
        name="peer_v",
    )(row8, upper, coef, v_tbl)


def _layer_norm(y, g, b):
    mu = jnp.mean(y, axis=-1, keepdims=True)
    d = y - mu
    var = jnp.mean(d * d, axis=-1, keepdims=True)
    return d * lax.rsqrt(var + LN_EPS) * g + b


def _store_tiles(o8_ref, out):
    tm = out.shape[0]
    for c in range(D_MODEL // LANES):
        o8_ref[pl.ds(c, tm, stride=SUBLANES), :] = out[:, c * LANES:(c + 1) * LANES]


def _add_ln_kernel(h_ref, f8_ref, g_ref, b_ref, o_ref):
    tm = h_ref.shape[0]
    f = jnp.concatenate([f8_ref[pl.ds(c, tm, stride=SUBLANES), :] for c in range(D_MODEL // LANES)], axis=1)
    o_ref[...] = _layer_norm(DN_ALPHA * h_ref[...] + f, g_ref[...], b_ref[...])


def _add_ln(h, f8, g, b):
    t, d = h.shape
    tb = ROW_TILE
    row = pl.BlockSpec((tb, d), lambda i: (i, 0))
    vec = pl.BlockSpec((1, d), lambda i: (0, 0))
    return pl.pallas_call(
        _add_ln_kernel,
        grid=(t // tb,),
        in_specs=[row, pl.BlockSpec((tb * SUBLANES, LANES), lambda i: (i, 0)), vec, vec],
        out_specs=row,
        out_shape=jax.ShapeDtypeStruct((t, d), F32),
        compiler_params=pltpu.CompilerParams(dimension_semantics=("arbitrary",)),
        name="add_ln",
    )(h, f8, g.reshape(1, d), b.reshape(1, d))


def _peer_block(h, h8, w_q, sub_keys, u_all, v_all, layer, ln_g, ln_b):
    row8, upper, gate = _peer_select(h, w_q, sub_keys)
    f8 = _peer_retrieve(h8, row8, upper, gate, _pack_table(u_all, layer), _pack_table(v_all, layer))
    return _add_ln(h, f8, ln_g, ln_b)


def _proj_ln_kernel(a_ref, w_ref, h_ref, g_ref, b_ref, o_ref, o8_ref):
    y = DN_ALPHA * h_ref[...] + jnp.dot(a_ref[...], w_ref[...], preferred_element_type=F32)
    out = _layer_norm(y, g_ref[...], b_ref[...])
    o_ref[...] = out
    _store_tiles(o8_ref, out)


def _proj_ln(a, w, h, g, b):
    t, k = a.shape
    d = w.shape[1]
    tm = ROW_TILE
    vec = pl.BlockSpec((1, d), lambda i: (0, 0))
    return pl.pallas_call(
        _proj_ln_kernel,
        grid=(t // tm,),
        in_specs=[pl.BlockSpec((tm, k), lambda i: (i, 0)), pl.BlockSpec((k, d), lambda i: (0, 0)),
                  pl.BlockSpec((tm, d), lambda i: (i, 0)), vec, vec],
        out_specs=[pl.BlockSpec((tm, d), lambda i: (i, 0)),
                   pl.BlockSpec((tm * SUBLANES, LANES), lambda i: (i, 0))],
        out_shape=[jax.ShapeDtypeStruct((t, d), F32), jax.ShapeDtypeStruct((t * SUBLANES, LANES), F32)],
        compiler_params=pltpu.CompilerParams(dimension_semantics=("arbitrary",),
                                             vmem_limit_bytes=VMEM_LIMIT_BYTES),
        name="proj_ln",
    )(a, w.astype(BF16), h, g.reshape(1, d), b.reshape(1, d))


RET_HEADS = 4
RET_DK = 256
RET_DV = 512
RET_HK = RET_HEADS * RET_DK
RET_HV = RET_HEADS * RET_DV
RET_CHUNK = 256
ROPE_BASE = 10000.0
RET_COL_TILE = 256


def _ret_in_kernel(x_ref, pos_ref, freq_ref, w_ref, qkv_ref, gate_ref):
    tn = RET_COL_TILE
    n_qk = 2 * RET_HK // tn
    n_qkv = n_qk + RET_HV // tn
    half = RET_DK // 2
    x = x_ref[...].astype(BF16)
    ang = pos_ref[...].astype(F32) * freq_ref[...]
    c, s = jnp.cos(ang), jnp.sin(ang)
    for j in range(w_ref.shape[1] // tn):
        y = jnp.dot(x, w_ref[:, j * tn:(j + 1) * tn], preferred_element_type=F32)
        if j < n_qk:
            x1, x2 = y[:, :half], y[:, half:]
            scale = 1.0 if j < n_qk // 2 else RET_DK ** -0.5
            qkv_ref[:, j * tn:j * tn + half] = ((x1 * c - x2 * s) * scale).astype(BF16)
            qkv_ref[:, j * tn + half:(j + 1) * tn] = ((x1 * s + x2 * c) * scale).astype(BF16)
        elif j < n_qkv:
            qkv_ref[:, j * tn:(j + 1) * tn] = y.astype(BF16)
        else:
            gate_ref[:, (j - n_qkv) * tn:(j - n_qkv + 1) * tn] = y


def _ret_in(x, pos, w_in):
    t, d = x.shape
    tm = ROW_TILE
    n_all = w_in.shape[1]
    n_qkv = 2 * RET_HK + RET_HV
    half = RET_DK // 2
    freq = (ROPE_BASE ** (-jnp.arange(0, RET_DK, 2, dtype=F32) / RET_DK)).reshape(1, half)
    return pl.pallas_call(
        _ret_in_kernel,
        grid=(t // tm,),
        in_specs=[pl.BlockSpec((tm, d), lambda i: (i, 0)),
                  pl.BlockSpec((tm, 1), lambda i: (i, 0)),
                  pl.BlockSpec((1, half), lambda i: (0, 0)),
                  pl.BlockSpec((d, n_all), lambda i: (0, 0))],
        out_specs=[pl.BlockSpec((tm, n_qkv), lambda i: (i, 0)),
                   pl.BlockSpec((tm, n_all - n_qkv), lambda i: (i, 0))],
        out_shape=[jax.ShapeDtypeStruct((t, n_qkv), BF16),
                   jax.ShapeDtypeStruct((t, n_all - n_qkv), F32)],
        compiler_params=pltpu.CompilerParams(dimension_semantics=("arbitrary",),
                                             vmem_limit_bytes=VMEM_LIMIT_BYTES),
        name="ret_in",
    )(x, pos.reshape(t, 1), freq, w_in.astype(BF16))


def _ret_scan_kernel(q_ref, k_ref, v_ref, gate_ref, dec_ref, xi_ref, zeta_ref, gc_ref, gng_ref, gnb_ref,
                     o_ref, state_ref, y_ref):
    c = RET_CHUNK
    n_chunks = q_ref.shape[0] // c

    def chunk_out(i, d):
        rows = pl.ds(pl.multiple_of(i * c, c), c)
        qi, ki, vi = q_ref[rows, :], k_ref[rows, :], v_ref[rows, :]
        sc = lax.dot_general(qi, ki, (((1,), (1,)), ((), ())), preferred_element_type=F32)
        inner = jnp.dot((sc * dec_ref[d]).astype(BF16), vi, preferred_element_type=F32)
        cross = jnp.dot(qi, state_ref[...].astype(BF16), preferred_element_type=F32) * xi_ref[d]
        kz = (ki.astype(F32) * zeta_ref[d]).astype(BF16)
        upd = lax.dot_general(kz, vi, (((0,), (0,)), ((), ())), preferred_element_type=F32)
        state_ref[...] = gc_ref[d] * state_ref[...] + upd
        return rows, inner + cross

    state_ref[...] = jnp.zeros_like(state_ref)

    def fwd(i, carry):
        rows, y = chunk_out(i, 0)
        y_ref[rows, :] = y
        return carry

    lax.fori_loop(0, n_chunks, fwd, 0)
    state_ref[...] = jnp.zeros_like(state_ref)

    def bwd(n, carry):
        rows, y = chunk_out(n_chunks - 1 - n, 1)
        y = y + y_ref[rows, :]
        mu = jnp.mean(y, axis=-1, keepdims=True)
        dlt = y - mu
        var = jnp.mean(dlt * dlt, axis=-1, keepdims=True)
        yn = dlt * lax.rsqrt(var + LN_EPS) * gng_ref[...] + gnb_ref[...]
        g = gate_ref[rows, :]
        o_ref[rows, :] = (g * (1.0 / (1.0 + jnp.exp(-g))) * yn).astype(BF16)
        return carry

    lax.fori_loop(0, n_chunks, bwd, 0)


def _ret_scan(qkv, gate, log1m_decay, gn_g, gn_b, batch, seq):
    c = RET_CHUNK
    log_gamma = jnp.log1p(-jnp.exp(log1m_decay.astype(F32)))
    idx = jnp.arange(c, dtype=F32)
    diff = idx[:, None] - idx[None, :]
    lg = log_gamma[:, :, None, None]
    dec_f = jnp.where(diff >= 0, jnp.exp(jnp.maximum(diff, 0.0) * lg[0]), 0.0)
    dec_b = jnp.where(diff < 0, jnp.exp(jnp.maximum(-diff, 0.0) * lg[1]), 0.0)
    dec = jnp.stack([dec_f, dec_b], axis=1)
    lgc = log_gamma[:, :, None]
    xi = jnp.stack([jnp.exp((idx + 1.0) * lgc[0]), jnp.exp((c - idx) * lgc[1])], axis=1)[..., None]
    zeta = jnp.stack([jnp.exp((c - 1.0 - idx) * lgc[0]), jnp.exp(idx * lgc[1])], axis=1)[..., None]
    gc = jnp.exp(c * log_gamma).T.reshape(RET_HEADS, 2, 1, 1)
    t = batch * seq
    kq = RET_HK // RET_DK
    kv = 2 * RET_HK // RET_DV
    head4 = lambda b, h: (h, 0, 0, 0)
    return pl.pallas_call(
        _ret_scan_kernel,
        grid=(batch, RET_HEADS),
        in_specs=[pl.BlockSpec((seq, RET_DK), lambda b, h: (b, h)),
                  pl.BlockSpec((seq, RET_DK), lambda b, h: (b, kq + h)),
                  pl.BlockSpec((seq, RET_DV), lambda b, h: (b, kv + h)),
                  pl.BlockSpec((seq, RET_DV), lambda b, h: (b, h)),
                  pl.BlockSpec((None, 2, c, c), head4),
                  pl.BlockSpec((None, 2, c, 1), head4),
                  pl.BlockSpec((None, 2, c, 1), head4),
                  pl.BlockSpec((None, 2, 1, 1), head4),
                  pl.BlockSpec((1, RET_DV), lambda b, h: (0, h)),
                  pl.BlockSpec((1, RET_DV), lambda b, h: (0, h))],
        out_specs=pl.BlockSpec((seq, RET_DV), lambda b, h: (b, h)),
        out_shape=jax.ShapeDtypeStruct((t, RET_HV), BF16),
        scratch_shapes=[pltpu.VMEM((RET_DK, RET_DV), F32), pltpu.VMEM((seq, RET_DV), F32)],
        compiler_params=pltpu.CompilerParams(dimension_semantics=("arbitrary", "arbitrary"),
                                             vmem_limit_bytes=VMEM_LIMIT_BYTES),
        name="ret_scan",
    )(qkv, qkv, qkv, gate, dec, xi, zeta, gc, gn_g.reshape(1, RET_HV), gn_b.reshape(1, RET_HV))


def _retention_block(h, pos, w_in, log1m_decay, gn_g, gn_b, w_out, ln_g, ln_b, batch, seq):
    qkv, gate = _ret_in(h, pos, w_in)
    gated = _ret_scan(qkv, gate, log1m_decay, gn_g, gn_b, batch, seq)
    return _proj_ln(gated, w_out, h, ln_g, ln_b)


MLA_HEADS = 8
MLA_NOPE = 128
MLA_ROPE = 64
MLA_VDIM = 128
MLA_Q_RANK = 384
MLA_KV_RANK = 256
MLA_QK_PAD = 256
MLA_Q_TILE = 256


def _rms(x, g):
    ms = jnp.mean(x * x, axis=-1, keepdims=True)
    return x * lax.rsqrt(ms + LN_EPS) * g


def _mla_in_kernel(x_ref, pos_ref, freq_ref, win_ref, qg_ref, kvg_ref, wuq_ref, wukv_ref, q_ref, k_ref, v_ref):
    c = jnp.dot(x_ref[...].astype(BF16), win_ref[...], preferred_element_type=F32)
    lane = lax.broadcasted_iota(I32, (1, LANES), 1)
    half = MLA_ROPE // 2
    ang = pos_ref[...].astype(F32) * freq_ref[...]
    cosv = jnp.where(lane < MLA_ROPE, jnp.cos(ang), 0.0)
    sinv = jnp.sin(ang)
    sinv = jnp.where(lane < half, -sinv, jnp.where(lane < MLA_ROPE, sinv, 0.0))

    def rope(y):
        swapped = pltpu.roll(y, half, 1) + pltpu.roll(y, LANES - half, 1)
        return y * cosv + swapped * sinv

    cq = _rms(c[:, :MLA_Q_RANK], qg_ref[...]).astype(BF16)
    ckv = _rms(c[:, MLA_Q_RANK:MLA_Q_RANK + MLA_KV_RANK], kvg_ref[...]).astype(BF16)
    k_rope = rope(c[:, MLA_Q_RANK + MLA_KV_RANK:]).astype(BF16)
    q = jnp.dot(cq, wuq_ref[...], preferred_element_type=F32)
    kv = jnp.dot(ckv, wukv_ref[...], preferred_element_type=F32)
    for hd in range(MLA_HEADS):
        o = hd * MLA_QK_PAD
        q_ref[:, o:o + MLA_NOPE] = q[:, o:o + MLA_NOPE].astype(BF16)
        q_ref[:, o + MLA_NOPE:o + MLA_QK_PAD] = rope(q[:, o + MLA_NOPE:o + MLA_QK_PAD]).astype(BF16)
        k_ref[:, o:o + MLA_NOPE] = kv[:, o:o + MLA_NOPE].astype(BF16)
        k_ref[:, o + MLA_NOPE:o + MLA_QK_PAD] = k_rope
        v_ref[:, hd * MLA_VDIM:(hd + 1) * MLA_VDIM] = kv[:, o + MLA_NOPE:o + MLA_QK_PAD].astype(BF16)


def _mla_in(x, pos, w_in, q_norm_g, kv_norm_g, w_uq, w_ukv):
    t, d = x.shape
    tm = ROW_TILE
    pad = MLA_QK_PAD - MLA_NOPE - MLA_ROPE
    win = jnp.pad(w_in, ((0, 0), (0, pad))).astype(BF16)
    wuq = jnp.pad(w_uq.reshape(MLA_Q_RANK, MLA_HEADS, MLA_NOPE + MLA_ROPE), ((0, 0), (0, 0), (0, pad)))
    wuq = wuq.reshape(MLA_Q_RANK, MLA_HEADS * MLA_QK_PAD).astype(BF16)
    wukv = w_ukv.astype(BF16)
    f = ROPE_BASE ** (-jnp.arange(0, MLA_ROPE, 2, dtype=F32) / MLA_ROPE)
    freq = jnp.concatenate([f, f, jnp.zeros((LANES - MLA_ROPE,), F32)]).reshape(1, LANES)
    full = lambda a: pl.BlockSpec(a.shape, lambda i: (0,) * a.ndim)
    qg = q_norm_g.reshape(1, -1)
    kvg = kv_norm_g.reshape(1, -1)
    wide = MLA_HEADS * MLA_QK_PAD
    return pl.pallas_call(
        _mla_in_kernel,
        grid=(t // tm,),
        in_specs=[pl.BlockSpec((tm, d), lambda i: (i, 0)), pl.BlockSpec((tm, 1), lambda i: (i, 0)),
                  full(freq), full(win), full(qg), full(kvg), full(wuq), full(wukv)],
        out_specs=[pl.BlockSpec((tm, wide), lambda i: (i, 0)), pl.BlockSpec((tm, wide), lambda i: (i, 0)),
                   pl.BlockSpec((tm, MLA_HEADS * MLA_VDIM), lambda i: (i, 0))],
        out_shape=[jax.ShapeDtypeStruct((t, wide), BF16), jax.ShapeDtypeStruct((t, wide), BF16),
                   jax.ShapeDtypeStruct((t, MLA_HEADS * MLA_VDIM), BF16)],
        compiler_params=pltpu.CompilerParams(dimension_semantics=("arbitrary",),
                                             vmem_limit_bytes=VMEM_LIMIT_BYTES),
        name="mla_in",
    )(x, pos.reshape(t, 1), freq, win, qg, kvg, wuq, wukv)


def _mla_attn_kernel(q_ref, k_ref, v_ref, o_ref):
    s = lax.dot_general(q_ref[...], k_ref[...], (((1,), (1,)), ((), ())), preferred_element_type=F32)
    s = s * ((MLA_NOPE + MLA_ROPE) ** -0.5)
    p = jnp.exp(s - jnp.max(s, axis=-1, keepdims=True))
    denom = jnp.sum(p, axis=-1, keepdims=True)
    o = jnp.dot(p.astype(BF16), v_ref[...], preferred_element_type=F32)
    o_ref[...] = (o / denom).astype(BF16)


def _mla_attn(q, k, v, batch, seq):
    t = batch * seq
    tq = MLA_Q_TILE
    nq = seq // tq
    return pl.pallas_call(
        _mla_attn_kernel,
        grid=(batch, MLA_HEADS, nq),
        in_specs=[pl.BlockSpec((tq, MLA_QK_PAD), lambda b, h, i: (b * nq + i, h)),
                  pl.BlockSpec((seq, MLA_QK_PAD), lambda b, h, i: (b, h)),
                  pl.BlockSpec((seq, MLA_VDIM), lambda b, h, i: (b, h))],
        out_specs=pl.BlockSpec((tq, MLA_VDIM), lambda b, h, i: (b * nq + i, h)),
        out_shape=jax.ShapeDtypeStruct((t, MLA_HEADS * MLA_VDIM), BF16),
        compiler_params=pltpu.CompilerParams(dimension_semantics=("arbitrary", "arbitrary", "arbitrary"),
                                             vmem_limit_bytes=VMEM_LIMIT_BYTES),
        name="mla_attn",
    )(q, k, v)


def _mla_block(h, pos, w_in, q_norm_g, kv_norm_g, w_uq, w_ukv, w_out, ln_g, ln_b, batch, seq):
    q, k, v = _mla_in(h, pos, w_in, q_norm_g, kv_norm_g, w_uq, w_ukv)
    o = _mla_attn(q, k, v, batch, seq)
    return _proj_ln(o, w_out, h, ln_g, ln_b)


def kernel(x, positions, ret_w_in, ret_log1m_decay, ret_gn_g, ret_gn_b, ret_w_out, mla_w_in, mla_q_norm,
           mla_kv_norm, mla_w_uq, mla_w_ukv, mla_w_out, peer_w_q, peer_sub_keys, peer_u, peer_v, ln_mix_g,
           ln_mix_b, ln_ffn_g, ln_ffn_b):
    b, s, d = x.shape
    t = b * s
    h = x.reshape(t, d)
    pos = positions.reshape(t)
    for i in range(DEPTH):
        j = i // 2
        if i % 2 == 0:
            h, h8 = _retention_block(h, pos, ret_w_in[j], ret_log1m_decay[j], ret_gn_g[j], ret_gn_b[j],
                                     ret_w_out[j], ln_mix_g[i], ln_mix_b[i], b, s)
        else:
            h, h8 = _mla_block(h, pos, mla_w_in[j], mla_q_norm[j], mla_kv_norm[j], mla_w_uq[j], mla_w_ukv[j],
                               mla_w_out[j], ln_mix_g[i], ln_mix_b[i], b, s)
        h = _peer_block(h, h8, peer_w_q[i], peer_sub_keys[i], peer_u, peer_v, i, ln_ffn_g[i], ln_ffn_b[i])
    return h.reshape(b, s, d)
```
